```python
import jax
import jax.numpy as jnp
from jax import lax
import numpy as np

D_MODEL = 2048
BATCH = 4
SEQ = 8192
DEPTH = 4
DEC_BATCH = 8
DEC_SEQ = 32
PAST_LEN = 2048

CHUNK = 64
D_MIX = D_MODEL
EPS = 1e-6
NEG_INF = -1e30
D_CONV = D_MIX // 4
CONV_W = 31
CONV_GROUPS = 8
D_ATT = D_MIX // 4
N_HEADS = 8
HEAD_DIM = D_ATT // N_HEADS
BAND_CHUNKS = 8
BAND_PAST = BAND_CHUNKS * CHUNK
BAND_LEN = (BAND_CHUNKS + 1) * CHUNK
REL_CLIP = 128
D_POOL = D_MIX // 4
POOL_WINDOWS = (2, 4, 8, 16)
POOL_GROUP = D_POOL // 4
POOL_HIST = 15
D_SG = D_MIX // 4
SG_CHUNK = 128
SG_GROUPS = 4
SG_GROUP = D_SG // SG_GROUPS
D_FF = -(-8 * D_MODEL // (3 * 256)) * 256
D_IN = 2 * D_CONV + 3 * D_ATT + D_POOL + 2 * D_SG
IN_SPLITS = (D_CONV, 2 * D_CONV, 2 * D_CONV + D_ATT, 2 * D_CONV + 2 * D_ATT,
             2 * D_CONV + 3 * D_ATT, 2 * D_CONV + 3 * D_ATT + D_POOL,
             2 * D_CONV + 3 * D_ATT + D_POOL + D_SG)

kernel_name = 'hybrid_streaming_encoder_step'


def rms_norm(x, g):
    xf = x.astype(jnp.float32)
    y = xf * lax.rsqrt(jnp.mean(xf * xf, axis=-1, keepdims=True) + EPS)
    return (y * g).astype(x.dtype)


def group_norm(x, g, b, groups):
    shp = x.shape
    xf = x.astype(jnp.float32).reshape(shp[:-1] + (groups, shp[-1] // groups))
    mu = jnp.mean(xf, axis=-1, keepdims=True)
    var = jnp.mean(jnp.square(xf - mu), axis=-1, keepdims=True)
    y = ((xf - mu) * lax.rsqrt(var + EPS)).reshape(shp)
    return (y * g + b).astype(x.dtype)


def conv_module(za, zg, hist, conv_w, conv_b, gn_g, gn_b, conv_pw):
    u = za * jax.nn.sigmoid(zg)
    ext = jnp.concatenate([hist.astype(u.dtype), u], axis=1)
    y = lax.conv_general_dilated(ext, conv_w[:, None, :], window_strides=(1,), padding='VALID',
                                 dimension_numbers=('NWC', 'WIO', 'NWC'),
                                 feature_group_count=D_CONV) + conv_b
    y = jax.nn.silu(group_norm(y, gn_g, gn_b, CONV_GROUPS)) @ conv_pw
    return y, ext[:, -(CONV_W - 1):]


def band_attn_prompt(q, k, v, rel_bias):
    B, T, H, Dh = q.shape
    nc = T // CHUNK
    qc = q.reshape(B, nc, CHUNK, H, Dh)
    pad = jnp.zeros((B, BAND_PAST, H, Dh), k.dtype)
    kp = jnp.concatenate([pad, k], axis=1).reshape(B, nc + BAND_CHUNKS, CHUNK, H, Dh)
    vp = jnp.concatenate([pad, v], axis=1).reshape(B, nc + BAND_CHUNKS, CHUNK, H, Dh)
    idx = jnp.arange(nc)[:, None] + jnp.arange(BAND_CHUNKS + 1)[None, :]
    kb = kp[:, idx].reshape(B, nc, BAND_LEN, H, Dh)
    vb = vp[:, idx].reshape(B, nc, BAND_LEN, H, Dh)
    j = jnp.arange(BAND_LEN)
    i = jnp.arange(CHUNK)
    rel = jnp.clip(j[None, :] - BAND_PAST - i[:, None], -REL_CLIP, REL_CLIP) + REL_CLIP
    bias = rel_bias[:, rel].astype(jnp.float32)
    valid = j[None, :] >= (BAND_CHUNKS - jnp.arange(nc))[:, None] * CHUNK
    s = jnp.einsum('bnqhd,bnkhd->bnhqk', qc, kb, preferred_element_type=jnp.float32) * (Dh ** -0.5) + bias
    s = jnp.where(valid[None, :, None, None, :], s, NEG_INF)
    p = jax.nn.softmax(s, axis=-1).astype(v.dtype)
    o = jnp.einsum('bnhqk,bnkhd->bnqhd', p, vb)
    return o.reshape(B, T, H * Dh)


def band_attn_sample(q, k, v, k_cache, v_cache, rel_bias):
    B, T, H, Dh = q.shape
    L = k_cache.shape[1]
    kk = jnp.concatenate([k_cache.astype(k.dtype), k], axis=1)
    vv = jnp.concatenate([v_cache.astype(v.dtype), v], axis=1)
    kpos = jnp.arange(L + T) - L
    rel = jnp.clip(kpos[None, :] - jnp.arange(T)[:, None], -REL_CLIP, REL_CLIP) + REL_CLIP
    bias = rel_bias[:, rel].astype(jnp.float32)
    s = jnp.einsum('bqhd,bkhd->bhqk', q, kk, preferred_element_type=jnp.float32) * (Dh ** -0.5) + bias
    p = jax.nn.softmax(s, axis=-1).astype(v.dtype)
    o = jnp.einsum('bhqk,bkhd->bqhd', p, vv)
    return o.reshape(B, T, H * Dh)


def pool_mixer(p, hist, pos0, pool_w, pool_scale):
    B, T, C = p.shape
    ext = jnp.concatenate([hist.astype(p.dtype), p], axis=1)
    extf = ext.astype(jnp.float32)
    cs = jnp.concatenate([jnp.zeros((B, 1, C), jnp.float32), lax.cumsum(extf, axis=1)], axis=1)
    pos = pos0 + jnp.arange(T)
    outs = []
    for gi, w in enumerate(POOL_WINDOWS):
        sl = slice(gi * POOL_GROUP, (gi + 1) * POOL_GROUP)
        win = cs[:, POOL_HIST + 1:POOL_HIST + 1 + T, sl] - cs[:, POOL_HIST + 1 - w:POOL_HIST + 1 - w + T, sl]
        cnt = jnp.minimum(w, pos + 1).astype(jnp.float32)[None, :, None]
        outs.append(win / cnt - extf[:, POOL_HIST:, sl])
    m = jnp.concatenate(outs, axis=-1).reshape(B, T, len(POOL_WINDOWS), POOL_GROUP).astype(p.dtype)
    y = jnp.einsum('btgc,gcd->btgd', m, pool_w).reshape(B, T, C) * pool_scale
    return y, ext[:, -POOL_HIST:]


def spatial_gating(u, v, ln_g, ln_b, sg_w, sg_b):
    B, T, C = v.shape
    vn = group_norm(v, ln_g, ln_b, 1)
    L = min(T, SG_CHUNK)
    n = T // L
    vc = vn.reshape(B, n, L, SG_GROUPS, SG_GROUP)
    w = sg_w[:, :L, :L] * jnp.tril(jnp.ones((L, L), sg_w.dtype))
    s = jnp.einsum('gij,bnjgc->bnigc', w, vc) + sg_b[:, :L].T[None, None, :, :, None]
    return u * s.reshape(B, T, C), vn


def trunk_layer(x, c, lp, conv_hist, pool_hist, k_cache, v_cache, pos0):
    (ada_w, ada_b, n1, n2, w_in, conv_w, conv_b, gn_g, gn_b, conv_pw, rel_bias,
     pool_w, pool_scale, sg_g, sg_bn, sg_w, sg_b, w_out, wg, wu, wd) = lp
    B, T, _ = x.shape
    mod = (jax.nn.silu(c) @ ada_w + ada_b)[:, None, :]
    sh1, sc1, g1, sh2, sc2, g2 = jnp.split(mod, 6, axis=-1)
    h = rms_norm(x, n1) * (1 + sc1) + sh1
    z = h @ w_in
    za, zg, zq, zk, zv, zp, zu, zs = jnp.split(z, IN_SPLITS, axis=-1)
    if conv_hist is None:
        conv_hist = jnp.zeros((B, CONV_W - 1, D_CONV), x.dtype)
        pool_hist = jnp.zeros((B, POOL_HIST, D_POOL), x.dtype)
    ya, conv_state = conv_module(za, zg, conv_hist, conv_w, conv_b, gn_g, gn_b, conv_pw)
    q = zq.reshape(B, T, N_HEADS, HEAD_DIM)
    k = zk.reshape(B, T, N_HEADS, HEAD_DIM)
    v = zv.reshape(B, T, N_HEADS, HEAD_DIM)
    if k_cache is None:
        yb = band_attn_prompt(q, k, v, rel_bias)
        k_rows, v_rows = k[:, -BAND_PAST:], v[:, -BAND_PAST:]
    else:
        yb = band_attn_sample(q, k, v, k_cache, v_cache, rel_bias)
        k_rows, v_rows = k, v
    yc, pool_state = pool_mixer(zp, pool_hist, pos0, pool_w, pool_scale)
    yd, sg_v = spatial_gating(zu, zs, sg_g, sg_bn, sg_w, sg_b)
    mix = jnp.concatenate([ya, yb, yc, yd], axis=-1) @ w_out
    x = x + g1 * mix
    h2 = rms_norm(x, n2) * (1 + sc2) + sh2
    x = x + g2 * ((jax.nn.silu(h2 @ wg) * (h2 @ wu)) @ wd)
    return x, (conv_state, k_rows, v_rows, pool_state, sg_v)


def setup_inputs(seed: int = 0) -> dict:
    key = jax.random.key(seed)
    ks = iter(jax.random.split(key, 40))

    def nrm(shape, s):
        return jax.random.normal(next(ks), shape, jnp.float32) * s

    kv_len = min(BAND_PAST, PAST_LEN)
    return {
        'x_prompt': nrm((BATCH, SEQ, D_MODEL), 1.0),
        'x_sample': nrm((DEC_BATCH, DEC_SEQ, D_MODEL), 1.0),
        'c_prompt': nrm((BATCH, D_MODEL), 1.0),
        'c_sample': nrm((DEC_BATCH, D_MODEL), 1.0),
        'cache_conv': nrm((DEPTH, DEC_BATCH, CONV_W - 1, D_CONV), 0.5),
        'cache_k': nrm((DEPTH, DEC_BATCH, kv_len, N_HEADS, HEAD_DIM), 1.0),
        'cache_v': nrm((DEPTH, DEC_BATCH, kv_len, N_HEADS, HEAD_DIM), 1.0),
        'cache_pool': nrm((DEPTH, DEC_BATCH, POOL_HIST, D_POOL), 1.0),
        'ada_w': nrm((DEPTH, D_MODEL, 6 * D_MODEL), 0.5 * D_MODEL ** -0.5),
        'ada_b': nrm((DEPTH, 6 * D_MODEL), 0.02),
        'norm1_g': 1.0 + nrm((DEPTH, D_MODEL), 0.02),
        'norm2_g': 1.0 + nrm((DEPTH, D_MODEL), 0.02),
        'w_in': nrm((DEPTH, D_MODEL, D_IN), D_MODEL ** -0.5),
        'conv_w': nrm((DEPTH, CONV_W, D_CONV), CONV_W ** -0.5),
        'conv_b': nrm((DEPTH, D_CONV), 0.02),
        'conv_gn_g': 1.0 + nrm((DEPTH, D_CONV), 0.02),
        'conv_gn_b': nrm((DEPTH, D_CONV), 0.02),
        'conv_pw': nrm((DEPTH, D_CONV, D_CONV), D_CONV ** -0.5),
        'rel_bias': nrm((DEPTH, N_HEADS, 2 * REL_CLIP + 1), 0.5),
        'pool_w': nrm((DEPTH, len(POOL_WINDOWS), POOL_GROUP, POOL_GROUP), POOL_GROUP ** -0.5),
        'pool_scale': 1.0 + nrm((DEPTH, D_POOL), 0.1),
        'sg_ln_g': 1.0 + nrm((DEPTH, D_SG), 0.02),
        'sg_ln_b': nrm((DEPTH, D_SG), 0.02),
        'sg_w': nrm((DEPTH, SG_GROUPS, SG_CHUNK, SG_CHUNK), 0.5 * SG_CHUNK ** -0.5),
        'sg_b': 1.0 + nrm((DEPTH, SG_GROUPS, SG_CHUNK), 0.1),
        'w_out': nrm((DEPTH, D_MIX, D_MODEL), D_MIX ** -0.5),
        'ffn_gate': nrm((DEPTH, D_MODEL, D_FF), D_MODEL ** -0.5),
        'ffn_up': nrm((DEPTH, D_MODEL, D_FF), D_MODEL ** -0.5),
        'ffn_down': nrm((DEPTH, D_FF, D_MODEL), D_FF ** -0.5),
        'final_g': 1.0 + nrm((D_MODEL,), 0.02),
    }


def reference(x_prompt, x_sample, c_prompt, c_sample, cache_conv, cache_k, cache_v, cache_pool,
              ada_w, ada_b, norm1_g, norm2_g, w_in, conv_w, conv_b, conv_gn_g, conv_gn_b, conv_pw,
              rel_bias, pool_w, pool_scale, sg_ln_g, sg_ln_b, sg_w, sg_b, w_out,
              ffn_gate, ffn_up, ffn_down, final_g):
    xp, xs = x_prompt, x_sample
    conv_p, conv_s, kp_l, vp_l, ks_l, vs_l, pool_p, pool_s, sgv_s = [], [], [], [], [], [], [], [], []
    for l in range(DEPTH):
        lp = (ada_w[l], ada_b[l], norm1_g[l], norm2_g[l], w_in[l], conv_w[l], conv_b[l],
              conv_gn_g[l], conv_gn_b[l], conv_pw[l], rel_bias[l], pool_w[l], pool_scale[l],
              sg_ln_g[l], sg_ln_b[l], sg_w[l], sg_b[l], w_out[l], ffn_gate[l], ffn_up[l], ffn_down[l])
        xp, (cp, kr, vr, pp, _) = trunk_layer(xp, c_prompt, lp, None, None, None, None, 0)
        xs, (cs_, ksr, vsr, ps, sgv) = trunk_layer(xs, c_sample, lp, cache_conv[l], cache_pool[l],
                                                   cache_k[l], cache_v[l], PAST_LEN)
        conv_p.append(cp)
        kp_l.append(kr)
        vp_l.append(vr)
        pool_p.append(pp)
        conv_s.append(cs_)
        ks_l.append(ksr)
        vs_l.append(vsr)
        pool_s.append(ps)
        sgv_s.append(sgv)
    y_prompt = rms_norm(xp, final_g)
    y_sample = rms_norm(xs, final_g)
    return (y_prompt, y_sample,
            jnp.stack(conv_p), jnp.stack(conv_s),
            jnp.stack(kp_l), jnp.stack(vp_l), jnp.stack(ks_l), jnp.stack(vs_l),
            jnp.stack(pool_p), jnp.stack(pool_s), jnp.stack(sgv_s))
```

```python
import functools

import jax
import jax.numpy as jnp
from jax import lax
from jax.experimental import pallas as pl
from jax.experimental.pallas import tpu as pltpu

F32 = jnp.float32
BF16 = jnp.bfloat16

EPS = 1e-6
NEG_INF = -1e30
CHUNK = 64
BAND_PAST = 512
REL_CLIP = 128
N_HEADS = 8
CONV_W = 31
CONV_GROUPS = 8
POOL_WINDOWS = (2, 4, 8, 16)
POOL_HIST = 15
SG_CHUNK = 128
SG_GROUPS = 4
LANES = 128
CONV_HIST_PAD = 32
POOL_HIST_PAD = 16
CONV_ROWS = 64
VMEM_LIMIT = 56 * 1024 * 1024
PAST_LEN = 2048


def _params(n_axes, vmem=VMEM_LIMIT):
    return pltpu.CompilerParams(dimension_semantics=("arbitrary",) * n_axes, vmem_limit_bytes=vmem)


def _sigmoid(x):
    return 1.0 / (1.0 + jnp.exp(-x))


def _rms(x, g):
    return x * lax.rsqrt(jnp.mean(x * x, axis=-1, keepdims=True) + EPS) * g


def _mod_kernel(c_ref, w_ref, b_ref, o_ref):
    c = c_ref[...]
    a = (c * _sigmoid(c)).astype(BF16)
    o_ref[0] = jnp.dot(a, w_ref[0].astype(BF16), preferred_element_type=F32) + b_ref[0]


def _modulation(c_all, ada_w, ada_b, tn=1536):
    depth, d, n = ada_w.shape
    rows = c_all.shape[0]
    return pl.pallas_call(
        _mod_kernel,
        grid=(depth, n // tn),
        in_specs=[pl.BlockSpec((rows, d), lambda l, j: (0, 0)),
                  pl.BlockSpec((1, d, tn), lambda l, j: (l, 0, j)),
                  pl.BlockSpec((1, 1, tn), lambda l, j: (l, 0, j))],
        out_specs=pl.BlockSpec((1, rows, tn), lambda l, j: (l, 0, j)),
        out_shape=jax.ShapeDtypeStruct((depth, rows, n), F32),
        compiler_params=_params(2),
        name="modulation",
    )(c_all, ada_w, ada_b.reshape(depth, 1, n))


def _inproj_kernel(x_ref, g_ref, sc_ref, sh_ref, w_ref, z_ref, h_ref):
    @pl.when(pl.program_id(2) == 0)
    def _():
        h = _rms(x_ref[0], g_ref[...]) * (1.0 + sc_ref[0]) + sh_ref[0]
        h_ref[...] = h.astype(BF16)

    z_ref[0] = jnp.dot(h_ref[...], w_ref[...], preferred_element_type=F32)


def _mod_spec(mod, tm):
    if mod.shape[1] == 1:
        return pl.BlockSpec((1, 1, mod.shape[2]), lambda b, t, *_: (b, 0, 0))
    return pl.BlockSpec((1, tm, mod.shape[2]), lambda b, t, *_: (b, t, 0))


def _in_proj(x, g, sc, sh, w, tm, tn=1024):
    bsz, t, d = x.shape
    n = w.shape[1]
    return pl.pallas_call(
        _inproj_kernel,
        grid=(bsz, t // tm, n // tn),
        in_specs=[pl.BlockSpec((1, tm, d), lambda b, i, j: (b, i, 0)),
                  pl.BlockSpec((1, d), lambda b, i, j: (0, 0)),
                  _mod_spec(sc, tm), _mod_spec(sh, tm),
                  pl.BlockSpec((d, tn), lambda b, i, j: (0, j))],
        out_specs=pl.BlockSpec((1, tm, tn), lambda b, i, j: (b, i, j)),
        out_shape=jax.ShapeDtypeStruct((bsz, t, n), F32),
        scratch_shapes=[pltpu.VMEM((tm, d), BF16)],
        compiler_params=_params(3),
        name="in_proj",
    )(x, g, sc, sh, w)


def _group_mean(v, avg_ref):
    hi = v.astype(BF16)
    lo = (v - hi.astype(F32)).astype(BF16)
    a = avg_ref[...]
    return jnp.dot(hi, a, preferred_element_type=F32) + jnp.dot(lo, a, preferred_element_type=F32)


def _conv_kernel(za_ref, zg_ref, hist_ref, w_ref, b_ref, gg_ref, gb_ref, avg_ref, pw_ref,
                 y_ref, state_ref, ext_ref, acc_ref):
    t = pl.program_id(1)
    tm = za_ref.shape[1]
    c = za_ref.shape[2]
    hist = CONV_W - 1
    pad = CONV_HIST_PAD - hist

    @pl.when(t == 0)
    def _():
        ext_ref[0:pad, :] = jnp.zeros((pad, c), F32)
        ext_ref[pad:CONV_HIST_PAD, :] = hist_ref[0]

    @pl.when(t > 0)
    def _():
        ext_ref[0:CONV_HIST_PAD, :] = ext_ref[tm:tm + CONV_HIST_PAD, :]

    ext_ref[CONV_HIST_PAD:CONV_HIST_PAD + tm, :] = za_ref[0] * _sigmoid(zg_ref[0])

    rows = min(CONV_ROWS, tm)
    for r in range(tm // rows):
        acc = jnp.broadcast_to(b_ref[...], (rows, c))
        for j in range(CONV_W):
            acc = acc + ext_ref[r * rows + pad + j:r * rows + pad + j + rows, :] * w_ref[j:j + 1, :]
        acc_ref[r * rows:(r + 1) * rows, :] = acc

    y = acc_ref[...]
    mu = _group_mean(y, avg_ref)
    dlt = y - mu
    var = _group_mean(dlt * dlt, avg_ref)
    yn = dlt * lax.rsqrt(var + EPS) * gg_ref[...] + gb_ref[...]
    s = yn * _sigmoid(yn)
    y_ref[0] = jnp.dot(s.astype(BF16), pw_ref[...], preferred_element_type=F32).astype(y_ref.dtype)

    @pl.when(t == pl.num_programs(1) - 1)
    def _():
        state_ref[0] = ext_ref[tm + pad:tm + CONV_HIST_PAD, :]


def _conv_mixer(z, hist, conv_w, conv_b, gn_g, gn_b, avg, conv_pw, tm):
    bsz, t, _ = z.shape
    c = conv_w.shape[1]
    full = lambda shape: pl.BlockSpec(shape, lambda b, i: (0,) * len(shape))
    return pl.pallas_call(
        _conv_kernel,
        grid=(bsz, t // tm),
        in_specs=[pl.BlockSpec((1, tm, c), lambda b, i: (b, i, 0)),
                  pl.BlockSpec((1, tm, c), lambda b, i: (b, i, 1)),
                  pl.BlockSpec((1, CONV_W - 1, c), lambda b, i: (b, 0, 0)),
                  full((CONV_W, c)), full((1, c)), full((1, c)), full((1, c)),
                  full((c, c)), full((c, c))],
        out_specs=[pl.BlockSpec((1, tm, c), lambda b, i: (b, i, 0)),
                   pl.BlockSpec((1, CONV_W - 1, c), lambda b, i: (b, 0, 0))],
        out_shape=[jax.ShapeDtypeStruct((bsz, t, c), BF16),
                   jax.ShapeDtypeStruct((bsz, CONV_W - 1, c), F32)],
        scratch_shapes=[pltpu.VMEM((tm + CONV_HIST_PAD, c), F32), pltpu.VMEM((tm, c), F32)],
        compiler_params=_params(2),
        name="conv_mixer",
    )(z, z, hist, conv_w, conv_b, gn_g, gn_b, avg, conv_pw)


def _attn_kernel(q_ref, kp_ref, kc_ref, vp_ref, vc_ref, bias_ref, o_ref, *, tq, mask_first):
    m = pl.program_id(1)
    nq = q_ref.shape[1] // tq
    past = kp_ref.shape[1]
    win = past + tq
    hd2 = LANES // 2
    lane = lax.broadcasted_iota(jnp.int32, (1, LANES), 1)
    col = lax.broadcasted_iota(jnp.int32, (1, win), 1)
    for hp in range(q_ref.shape[2] // LANES):
        cs = slice(hp * LANES, (hp + 1) * LANES)
        q = q_ref[0, :, cs].astype(BF16)
        kk = jnp.concatenate([kp_ref[0, :, cs], kc_ref[0, :, cs]], axis=0).astype(BF16)
        vv = jnp.concatenate([vp_ref[0, :, cs], vc_ref[0, :, cs]], axis=0).astype(BF16)
        for c in range(nq):
            kw = kk[c * tq:c * tq + win]
            vw = vv[c * tq:c * tq + win]
            qc = q[c * tq:(c + 1) * tq]
            o = jnp.zeros((tq, LANES), F32)
            for a in range(2):
                sel = (lane >= a * hd2) & (lane < (a + 1) * hd2)
                qa = jnp.where(sel, qc, jnp.zeros_like(qc))
                va = jnp.where(sel, vw, jnp.zeros_like(vw))
                s = lax.dot_general(qa, kw, (((1,), (1,)), ((), ())), preferred_element_type=F32)
                s = s * (hd2 ** -0.5) + bias_ref[2 * hp + a]
                if mask_first:
                    valid = (col >= past - c * tq) | (m > 0)
                    s = jnp.where(valid, s, NEG_INF)
                s = s - jnp.max(s, axis=-1, keepdims=True)
                p = jnp.exp(s)
                p = p / jnp.sum(p, axis=-1, keepdims=True)
                o = o + jnp.dot(p.astype(BF16), va, preferred_element_type=F32)
            o_ref[0, c * tq:(c + 1) * tq, cs] = o.astype(o_ref.dtype)


def _attention(zq, q_col, k_new, kn_col, v_new, vn_col, k_past, kp_col, v_past, vp_col, bias, tq, nq,
               mask_first):
    bsz, t, _ = zq.shape
    c = N_HEADS * (LANES // 2)
    tb = tq * nq
    if mask_first:
        past_map = lambda col: (lambda b, m: (b, jnp.maximum(m - 1, 0), col))
    else:
        past_map = lambda col: (lambda b, m: (b, 0, col))
    return pl.pallas_call(
        functools.partial(_attn_kernel, tq=tq, mask_first=mask_first),
        grid=(bsz, t // tb),
        in_specs=[pl.BlockSpec((1, tb, c), lambda b, m: (b, m, q_col)),
                  pl.BlockSpec((1, BAND_PAST, c), past_map(kp_col)),
                  pl.BlockSpec((1, tb, c), lambda b, m: (b, m, kn_col)),
                  pl.BlockSpec((1, BAND_PAST, c), past_map(vp_col)),
                  pl.BlockSpec((1, tb, c), lambda b, m: (b, m, vn_col)),
                  pl.BlockSpec(bias.shape, lambda b, m: (0, 0, 0))],
        out_specs=pl.BlockSpec((1, tb, c), lambda b, m: (b, m, 0)),
        out_shape=jax.ShapeDtypeStruct((bsz, t, c), BF16),
        compiler_params=_params(2),
        name="band_attention",
    )(zq, k_past, k_new, v_past, v_new, bias)


def _pool_kernel(p_ref, hist_ref, w_ref, scale_ref, y_ref, ext_ref, *, pos0):
    t = pl.program_id(1)
    tm = p_ref.shape[1]
    c = p_ref.shape[2]
    pad = POOL_HIST_PAD - POOL_HIST

    @pl.when(t == 0)
    def _():
        ext_ref[0:pad, :] = jnp.zeros((pad, c), F32)
        ext_ref[pad:POOL_HIST_PAD, :] = hist_ref[0]

    @pl.when(t > 0)
    def _():
        ext_ref[0:POOL_HIST_PAD, :] = ext_ref[tm:tm + POOL_HIST_PAD, :]

    ext_ref[POOL_HIST_PAD:POOL_HIST_PAD + tm, :] = p_ref[0]
    pos = pos0 + t * tm + lax.broadcasted_iota(jnp.int32, (tm, 1), 0)
    gw = c // len(POOL_WINDOWS)
    for gi, w in enumerate(POOL_WINDOWS):
        cs = slice(gi * gw, (gi + 1) * gw)
        cur = ext_ref[POOL_HIST_PAD:POOL_HIST_PAD + tm, cs]
        win = cur
        for i in range(1, w):
            win = win + ext_ref[POOL_HIST_PAD - i:POOL_HIST_PAD - i + tm, cs]
        cnt = jnp.minimum(w, pos + 1).astype(F32)
        mgrp = (win / cnt - cur).astype(BF16)
        y = jnp.dot(mgrp, w_ref[gi], preferred_element_type=F32) * scale_ref[:, cs]
        y_ref[0, :, cs] = y.astype(y_ref.dtype)


def _pool_mixer(z, col, hist, pool_w, pool_scale, pos0, tm):
    bsz, t, _ = z.shape
    c = pool_scale.shape[1]
    return pl.pallas_call(
        functools.partial(_pool_kernel, pos0=pos0),
        grid=(bsz, t // tm),
        in_specs=[pl.BlockSpec((1, tm, c), lambda b, i: (b, i, col)),
                  pl.BlockSpec((1, POOL_HIST, c), lambda b, i: (b, 0, 0)),
                  pl.BlockSpec(pool_w.shape, lambda b, i: (0, 0, 0)),
                  pl.BlockSpec((1, c), lambda b, i: (0, 0))],
        out_specs=pl.BlockSpec((1, tm, c), lambda b, i: (b, i, 0)),
        out_shape=jax.ShapeDtypeStruct((bsz, t, c), BF16),
        scratch_shapes=[pltpu.VMEM((tm + POOL_HIST_PAD, c), F32)],
        compiler_params=_params(2),
        name="pool_mixer",
    )(z, hist, pool_w, pool_scale)


def _sg_kernel(u_ref, v_ref, g_ref, b_ref, w_ref, bias_ref, y_ref, *vn_out):
    tm = u_ref.shape[1]
    c = u_ref.shape[2]
    span = w_ref.shape[1]
    gw = c // SG_GROUPS
    v = v_ref[0]
    mu = jnp.mean(v, axis=-1, keepdims=True)
    dlt = v - mu
    var = jnp.mean(dlt * dlt, axis=-1, keepdims=True)
    vn = dlt * lax.rsqrt(var + EPS) * g_ref[...] + b_ref[...]
    if vn_out:
        vn_out[0][0] = vn
    vb = vn.astype(BF16)
    row = lax.broadcasted_iota(jnp.int32, (span, span), 0)
    colm = lax.broadcasted_iota(jnp.int32, (span, span), 1)
    for g in range(SG_GROUPS):
        cs = slice(g * gw, (g + 1) * gw)
        wg = jnp.where(colm <= row, w_ref[g], 0.0).astype(BF16)
        for ci in range(tm // span):
            rs = slice(ci * span, (ci + 1) * span)
            s = jnp.dot(wg, vb[rs, cs], preferred_element_type=F32) + bias_ref[:, cs]
            y_ref[0, rs, cs] = (u_ref[0, rs, cs] * s).astype(y_ref.dtype)


def _sg_mixer(z, u_col, v_col, ln_g, ln_b, sg_w, bias_full, tm, emit_vn):
    bsz, t, _ = z.shape
    c = ln_g.shape[1]
    out_specs = [pl.BlockSpec((1, tm, c), lambda b, i: (b, i, 0))]
    out_shape = [jax.ShapeDtypeStruct((bsz, t, c), BF16)]
    if emit_vn:
        out_specs.append(pl.BlockSpec((1, tm, c), lambda b, i: (b, i, 0)))
        out_shape.append(jax.ShapeDtypeStruct((bsz, t, c), F32))
    return pl.pallas_call(
        _sg_kernel,
        grid=(bsz, t // tm),
        in_specs=[pl.BlockSpec((1, tm, c), lambda b, i: (b, i, u_col)),
                  pl.BlockSpec((1, tm, c), lambda b, i: (b, i, v_col)),
                  pl.BlockSpec((1, c), lambda b, i: (0, 0)),
                  pl.BlockSpec((1, c), lambda b, i: (0, 0)),
                  pl.BlockSpec(sg_w.shape, lambda b, i: (0, 0, 0)),
                  pl.BlockSpec(bias_full.shape, lambda b, i: (0, 0))],
        out_specs=out_specs,
        out_shape=out_shape,
        compiler_params=_params(2),
        name="spatial_gating",
    )(z, z, ln_g, ln_b, sg_w, bias_full)


def _outproj_kernel(x_ref, ya_ref, yb_ref, yc_ref, yd_ref, g1_ref, w_ref, o_ref):
    mix = jnp.concatenate([ya_ref[0], yb_ref[0], yc_ref[0], yd_ref[0]], axis=-1)
    o_ref[0] = x_ref[0] + g1_ref[0] * jnp.dot(mix, w_ref[...], preferred_element_type=F32)


def _out_proj(x, ys, g1, w, tm):
    bsz, t, d = x.shape
    c = ys[0].shape[2]
    yspec = pl.BlockSpec((1, tm, c), lambda b, i: (b, i, 0))
    return pl.pallas_call(
        _outproj_kernel,
        grid=(bsz, t // tm),
        in_specs=[pl.BlockSpec((1, tm, d), lambda b, i: (b, i, 0)), yspec, yspec, yspec, yspec,
                  _mod_spec(g1, tm),
                  pl.BlockSpec(w.shape, lambda b, i: (0, 0))],
        out_specs=pl.BlockSpec((1, tm, d), lambda b, i: (b, i, 0)),
        out_shape=jax.ShapeDtypeStruct((bsz, t, d), F32),
        compiler_params=_params(2),
        name="out_proj",
    )(x, *ys, g1, w)


def _ffn_kernel(x_ref, g_ref, sc_ref, sh_ref, g2_ref, wg_ref, wu_ref, wd_ref, fg_ref, o_ref, h_ref, acc_ref,
                *, final_norm):
    f = pl.program_id(2)

    @pl.when(f == 0)
    def _():
        h = _rms(x_ref[0], g_ref[...]) * (1.0 + sc_ref[0]) + sh_ref[0]
        h_ref[...] = h.astype(BF16)
        acc_ref[...] = jnp.zeros_like(acc_ref)

    h = h_ref[...]
    gate = jnp.dot(h, wg_ref[...], preferred_element_type=F32)
    up = jnp.dot(h, wu_ref[...], preferred_element_type=F32)
    act = (gate * _sigmoid(gate) * up).astype(BF16)
    acc_ref[...] += jnp.dot(act, wd_ref[...], preferred_element_type=F32)

    @pl.when(f == pl.num_programs(2) - 1)
    def _():
        y = x_ref[0] + g2_ref[0] * acc_ref[...]
        if final_norm:
            y = _rms(y, fg_ref[...])
        o_ref[0] = y


def _ffn(x, g, sc, sh, g2, wg, wu, wd, fg, tm, tf, final_norm):
    bsz, t, d = x.shape
    ff = wg.shape[1]
    return pl.pallas_call(
        functools.partial(_ffn_kernel, final_norm=final_norm),
        grid=(bsz, t // tm, ff // tf),
        in_specs=[pl.BlockSpec((1, tm, d), lambda b, i, f: (b, i, 0)),
                  pl.BlockSpec((1, d), lambda b, i, f: (0, 0)),
                  _mod_spec(sc, tm), _mod_spec(sh, tm), _mod_spec(g2, tm),
                  pl.BlockSpec((d, tf), lambda b, i, f: (0, f)),
                  pl.BlockSpec((d, tf), lambda b, i, f: (0, f)),
                  pl.BlockSpec((tf, d), lambda b, i, f: (f, 0)),
                  pl.BlockSpec((1, d), lambda b, i, f: (0, 0))],
        out_specs=pl.BlockSpec((1, tm, d), lambda b, i, f: (b, i, 0)),
        out_shape=jax.ShapeDtypeStruct((bsz, t, d), F32),
        scratch_shapes=[pltpu.VMEM((tm, d), BF16), pltpu.VMEM((tm, d), F32)],
        compiler_params=_params(3),
        name="ffn",
    )(x, g, sc, sh, g2, wg, wu, wd, fg)


def _largest_tile(t, cap):
    tile = min(t, cap)
    while t % tile:
        tile //= 2
    return tile


def _rel_bias_table(rel_bias, tq):
    kpos = jnp.arange(BAND_PAST + tq) - BAND_PAST
    rel = jnp.clip(kpos[None, :] - jnp.arange(tq)[:, None], -REL_CLIP, REL_CLIP) + REL_CLIP
    return rel_bias[:, rel]


def kernel(x_prompt, x_sample, c_prompt, c_sample, cache_conv, cache_k, cache_v, cache_pool, ada_w, ada_b, norm1_g, norm2_g, w_in, conv_w, conv_b, conv_gn_g, conv_gn_b, conv_pw, rel_bias, pool_w, pool_scale, sg_ln_g, sg_ln_b, sg_w, sg_b, w_out, ffn_gate, ffn_up, ffn_down, final_g):
    depth = ada_w.shape[0]
    bp, tp, d = x_prompt.shape
    bs, ts, _ = x_sample.shape
    dc = conv_w.shape[2]
    assert tp % (CHUNK * 8) == 0 and ts <= CHUNK and ts <= SG_CHUNK and cache_k.shape[2] == BAND_PAST

    rows = bp + bs
    rows_pad = -(-rows // 8) * 8
    c_all = jnp.concatenate([c_prompt, c_sample, jnp.zeros((rows_pad - rows, d), F32)], axis=0)
    mod = _modulation(c_all, ada_w, ada_b)

    w_in_b, w_out_b = w_in.astype(BF16), w_out.astype(BF16)
    wg_b, wu_b, wd_b = ffn_gate.astype(BF16), ffn_up.astype(BF16), ffn_down.astype(BF16)
    pw_b, poolw_b = conv_pw.astype(BF16), pool_w.astype(BF16)
    gsz = dc // CONV_GROUPS
    gid = jnp.arange(dc) // gsz
    avg = jnp.where(gid[:, None] == gid[None, :], 1.0 / gsz, 0.0).astype(BF16)

    zero_conv = jnp.zeros((bp, CONV_W - 1, dc), F32)
    zero_pool = jnp.zeros((bp, POOL_HIST, dc), F32)
    cache_k2 = cache_k.reshape(depth, bs, BAND_PAST, dc)
    cache_v2 = cache_v.reshape(depth, bs, BAND_PAST, dc)

    tm_p = _largest_tile(tp, 512)
    ns = bs * ts
    xp = x_prompt
    xs = x_sample.reshape(1, ns, d)
    outs = {k: [] for k in ("conv_p", "conv_s", "k_p", "v_p", "k_s", "v_s", "pool_p", "pool_s", "sgv_s")}
    row = lambda a: a.reshape(1, -1)

    for l in range(depth):
        mp = [m.reshape(bp, 1, d) for m in jnp.split(mod[l, :bp], 6, axis=-1)]
        ms = [jnp.repeat(m, ts, axis=0).reshape(1, ns, d) for m in jnp.split(mod[l, bp:rows], 6, axis=-1)]
        n1, n2 = row(norm1_g[l]), row(norm2_g[l])
        bias_p = _rel_bias_table(rel_bias[l], CHUNK)
        bias_s = _rel_bias_table(rel_bias[l], ts)
        sgb_p = jnp.repeat(sg_b[l][:, :SG_CHUNK].T, dc // SG_GROUPS, axis=1)
        sgb_s = jnp.repeat(sg_b[l][:, :ts].T, dc // SG_GROUPS, axis=1)
        mixer_w = (conv_w[l], row(conv_b[l]), row(conv_gn_g[l]), row(conv_gn_b[l]), avg, pw_b[l])
        final = l == depth - 1

        z = _in_proj(xp, n1, mp[1], mp[0], w_in_b[l], tm_p)
        ya, conv_state = _conv_mixer(z, zero_conv, *mixer_w, tm=tm_p)
        yb = _attention(z, 2, z, 3, z, 4, z, 3, z, 4, bias_p, CHUNK, BAND_PAST // CHUNK, True)
        yc = _pool_mixer(z, 5, zero_pool, poolw_b[l], row(pool_scale[l]), 0, tm_p)
        (yd,) = _sg_mixer(z, 6, 7, row(sg_ln_g[l]), row(sg_ln_b[l]), sg_w[l], sgb_p, tm_p, False)
        xp = _out_proj(xp, (ya, yb, yc, yd), mp[2], w_out_b[l], tm_p)
        xp = _ffn(xp, n2, mp[4], mp[3], mp[5], wg_b[l], wu_b[l], wd_b[l], row(final_g), tm_p, 512, final)
        outs["conv_p"].append(conv_state)
        outs["k_p"].append(z[:, tp - BAND_PAST:, 3 * dc:4 * dc].reshape(bp, BAND_PAST, N_HEADS, -1))
        outs["v_p"].append(z[:, tp - BAND_PAST:, 4 * dc:5 * dc].reshape(bp, BAND_PAST, N_HEADS, -1))
        outs["pool_p"].append(z[:, tp - POOL_HIST:, 5 * dc:6 * dc])

        zs = _in_proj(xs, n1, ms[1], ms[0], w_in_b[l], ns).reshape(bs, ts, -1)
        ya, conv_state = _conv_mixer(zs, cache_conv[l], *mixer_w, tm=ts)
        yb = _attention(zs, 2, zs, 3, zs, 4, cache_k2[l], 0, cache_v2[l], 0, bias_s, ts, 1, False)
        yc = _pool_mixer(zs, 5, cache_pool[l], poolw_b[l], row(pool_scale[l]), PAST_LEN, ts)
        yd, sgv = _sg_mixer(zs, 6, 7, row(sg_ln_g[l]), row(sg_ln_b[l]), sg_w[l][:, :ts, :ts], sgb_s, ts, True)
        ys = tuple(y.reshape(1, ns, dc) for y in (ya, yb, yc, yd))
        xs = _out_proj(xs, ys, ms[2], w_out_b[l], ns)
        xs = _ffn(xs, n2, ms[4], ms[3], ms[5], wg_b[l], wu_b[l], wd_b[l], row(final_g), ns, 512, final)
        outs["conv_s"].append(conv_state)
        outs["k_s"].append(zs[:, :, 3 * dc:4 * dc].reshape(bs, ts, N_HEADS, -1))
        outs["v_s"].append(zs[:, :, 4 * dc:5 * dc].reshape(bs, ts, N_HEADS, -1))
        outs["pool_s"].append(jnp.concatenate([cache_pool[l], zs[:, :, 5 * dc:6 * dc]], axis=1)[:, -POOL_HIST:])
        outs["sgv_s"].append(sgv)

    st = {k: jnp.stack(v) for k, v in outs.items()}
    return (xp, xs.reshape(bs, ts, d), st["conv_p"], st["conv_s"], st["k_p"], st["v_p"], st["k_s"], st["v_s"],
            st["pool_p"], st["pool_s"], st["sgv_s"])
```

```python
import functools

import jax
import jax.numpy as jnp
from jax import lax
from jax.experimental import pallas as pl
from jax.experimental.pallas import tpu as pltpu

F32 = jnp.float32
BF16 = jnp.bfloat16

EPS = 1e-6
NEG_INF = -1e30
CHUNK = 64
BAND_PAST = 512
REL_CLIP = 128
N_HEADS = 8
CONV_W = 31
CONV_GROUPS = 8
POOL_WINDOWS = (2, 4, 8, 16)
POOL_HIST = 15
SG_CHUNK = 128
SG_GROUPS = 4
LANES = 128
SUBLANES = 8
CONV_HIST_PAD = 32
POOL_HIST_PAD = 16
CONV_ROWS = 64
VMEM_LIMIT = 56 * 1024 * 1024
PAST_LEN = 2048


def _params(n_axes, vmem=VMEM_LIMIT):
    return pltpu.CompilerParams(dimension_semantics=("arbitrary",) * n_axes, vmem_limit_bytes=vmem)


def _sigmoid(x):
    return 1.0 / (1.0 + jnp.exp(-x))


def _rms(x, g):
    return x * lax.rsqrt(jnp.mean(x * x, axis=-1, keepdims=True) + EPS) * g


def _mod_kernel(c_ref, w_ref, b_ref, o_ref):
    c = c_ref[...]
    a = (c * _sigmoid(c)).astype(BF16)
    o_ref[0] = jnp.dot(a, w_ref[0].astype(BF16), preferred_element_type=F32) + b_ref[0]


def _modulation(c_all, ada_w, ada_b, tn=1536):
    depth, d, n = ada_w.shape
    rows = c_all.shape[0]
    return pl.pallas_call(
        _mod_kernel,
        grid=(depth, n // tn),
        in_specs=[pl.BlockSpec((rows, d), lambda l, j: (0, 0)),
                  pl.BlockSpec((1, d, tn), lambda l, j: (l, 0, j)),
                  pl.BlockSpec((1, 1, tn), lambda l, j: (l, 0, j))],
        out_specs=pl.BlockSpec((1, rows, tn), lambda l, j: (l, 0, j)),
        out_shape=jax.ShapeDtypeStruct((depth, rows, n), F32),
        compiler_params=_params(2),
        name="modulation",
    )(c_all, ada_w, ada_b.reshape(depth, 1, n))


def _inproj_kernel(x_ref, g_ref, sc_ref, sh_ref, w_ref, z_ref, h_ref):
    i = pl.program_id(0)

    @pl.when(i == 0)
    def _():
        h_ref[1] = jnp.zeros(h_ref.shape[1:], BF16)

    for par in range(2):
        @pl.when(i % 2 == par)
        def _():
            h = _rms(x_ref[...], g_ref[...]) * (1.0 + sc_ref[0]) + sh_ref[0]
            h_ref[par] = h.astype(BF16)
            z_ref[...] = jnp.dot(h_ref[1 - par], w_ref[...], preferred_element_type=F32)


def _mod_spec(mod, tm):
    if mod.shape[1] == 1:
        return pl.BlockSpec((1, 1, mod.shape[2]), lambda b, t, *_: (b, 0, 0))
    return pl.BlockSpec((1, tm, mod.shape[2]), lambda b, t, *_: (b, t, 0))


def _mod_spec1(mod, tm, tiles_per_seq, n_tiles):
    tile = lambda i: jnp.minimum(i, n_tiles - 1)
    if mod.shape[1] == 1:
        return pl.BlockSpec((1, 1, mod.shape[2]), lambda i: (tile(i) // tiles_per_seq, 0, 0))
    return pl.BlockSpec((1, tm, mod.shape[2]), lambda i: (0, tile(i), 0))


def _in_proj(x, g, sc, sh, w, tm):
    bsz, t, d = x.shape
    n = w.shape[1]
    n_tiles = bsz * t // tm
    z = pl.pallas_call(
        _inproj_kernel,
        grid=(n_tiles + 1,),
        in_specs=[pl.BlockSpec((tm, d), lambda i: (jnp.minimum(i, n_tiles - 1), 0)),
                  pl.BlockSpec((1, d), lambda i: (0, 0)),
                  _mod_spec1(sc, tm, t // tm, n_tiles), _mod_spec1(sh, tm, t // tm, n_tiles),
                  pl.BlockSpec((d, n), lambda i: (0, 0), pipeline_mode=pl.Buffered(1))],
        out_specs=pl.BlockSpec((tm, n), lambda i: (jnp.maximum(i - 1, 0), 0)),
        out_shape=jax.ShapeDtypeStruct((bsz * t, n), F32),
        scratch_shapes=[pltpu.VMEM((2, tm, d), BF16)],
        compiler_params=_params(1),
        name="in_proj",
    )(x.reshape(bsz * t, d), g, sc, sh, w)
    return z.reshape(bsz, t, n)


def _group_mean(v, avg_ref):
    hi = v.astype(BF16)
    lo = (v - hi.astype(F32)).astype(BF16)
    a = avg_ref[...]
    return jnp.dot(hi, a, preferred_element_type=F32) + jnp.dot(lo, a, preferred_element_type=F32)


def _conv_kernel(za_ref, zg_ref, hist_ref, w_ref, b_ref, gg_ref, gb_ref, avg_ref, pw_ref,
                 y_ref, state_ref, ext_ref, acc_ref):
    t = pl.program_id(1)
    tm = za_ref.shape[1]
    c = za_ref.shape[2]
    hist = CONV_W - 1
    pad = CONV_HIST_PAD - hist

    @pl.when(t == 0)
    def _():
        ext_ref[0:pad, :] = jnp.zeros((pad, c), F32)
        ext_ref[pad:CONV_HIST_PAD, :] = hist_ref[0]

    @pl.when(t > 0)
    def _():
        ext_ref[0:CONV_HIST_PAD, :] = ext_ref[tm:tm + CONV_HIST_PAD, :]

    ext_ref[CONV_HIST_PAD:CONV_HIST_PAD + tm, :] = za_ref[0] * _sigmoid(zg_ref[0])

    rows = min(CONV_ROWS, tm)
    for r in range(tm // rows):
        base = r * rows
        acc = jnp.broadcast_to(b_ref[...], (rows, c))
        for phase in range(SUBLANES):
            offs = [o for o in range(pad, pad + CONV_W) if o % SUBLANES == phase]
            span = rows + (SUBLANES if phase else 0)
            grp = None
            for o in offs:
                lo = base + o - phase
                term = ext_ref[lo:lo + span, :] * w_ref[o - pad:o - pad + 1, :]
                grp = term if grp is None else grp + term
            acc = acc + grp[phase:phase + rows]
        acc_ref[base:base + rows, :] = acc

    y = acc_ref[...]
    mu = _group_mean(y, avg_ref)
    dlt = y - mu
    var = _group_mean(dlt * dlt, avg_ref)
    yn = dlt * lax.rsqrt(var + EPS) * gg_ref[...] + gb_ref[...]
    s = yn * _sigmoid(yn)
    y_ref[0] = jnp.dot(s.astype(BF16), pw_ref[...], preferred_element_type=F32).astype(y_ref.dtype)

    @pl.when(t == pl.num_programs(1) - 1)
    def _():
        state_ref[0] = ext_ref[tm + pad:tm + CONV_HIST_PAD, :]


def _attn_kernel(q_ref, kp_ref, kc_ref, vp_ref, vc_ref, e_ref, o_ref, bias_ref, s_ref, p_ref, *, tq, mask_first):
    first = (pl.program_id(0) == 0) & (pl.program_id(1) == 0)
    m = pl.program_id(1)
    nq = q_ref.shape[1] // tq
    past = kp_ref.shape[1]
    win = past + tq
    hd = LANES // 2
    elen = e_ref.shape[1]

    @pl.when(first)
    def _():
        for h in range(N_HEADS):
            rows = jnp.broadcast_to(e_ref[h:h + 1, :], (tq, elen))
            tz = pltpu.roll(rows, elen - (tq - 1), 1, stride=1, stride_axis=0)
            bias_ref[h // 2, (h % 2) * tq:(h % 2 + 1) * tq, :] = tz[:, :win]

    low = lax.broadcasted_iota(jnp.int32, (1, LANES), 1) < hd
    col = lax.broadcasted_iota(jnp.int32, (1, win), 1)

    def body(masked):
        for hp in range(q_ref.shape[2] // LANES):
            cs = slice(hp * LANES, (hp + 1) * LANES)
            buf = hp % 2
            q = (q_ref[0, :, cs] * (hd ** -0.5)).astype(BF16)
            qa = jnp.where(low, q, jnp.zeros_like(q))
            qb = jnp.where(low, jnp.zeros_like(q), q)
            kk = jnp.concatenate([kp_ref[0, :, cs], kc_ref[0, :, cs]], axis=0).astype(BF16)
            vv = jnp.concatenate([vp_ref[0, :, cs], vc_ref[0, :, cs]], axis=0).astype(BF16)
            for c in range(nq):
                q2 = jnp.concatenate([qa[c * tq:(c + 1) * tq], qb[c * tq:(c + 1) * tq]], axis=0)
                s = lax.dot_general(q2, kk[c * tq:c * tq + win], (((1,), (1,)), ((), ())),
                                    preferred_element_type=F32) + bias_ref[hp]
                if masked:
                    s = jnp.where(col >= past - c * tq, s, NEG_INF)
                s_ref[buf, c] = s
            inv = []
            for c in range(nq):
                s = s_ref[buf, c]
                p = jnp.exp(s - jnp.max(s, axis=-1, keepdims=True))
                inv.append(1.0 / jnp.sum(p, axis=-1, keepdims=True))
                p_ref[buf, c] = p.astype(BF16)
            for c in range(nq):
                o2 = jnp.dot(p_ref[buf, c], vv[c * tq:c * tq + win], preferred_element_type=F32) * inv[c]
                o = jnp.where(low, o2[:tq], o2[tq:])
                o_ref[0, c * tq:(c + 1) * tq, cs] = o.astype(o_ref.dtype)

    if mask_first:
        pl.when(m == 0)(functools.partial(body, True))
        pl.when(m > 0)(functools.partial(body, False))
    else:
        body(False)


def _attention(zq, q_col, k_new, kn_col, v_new, vn_col, k_past, kp_col, v_past, vp_col, evec, tq, nq,
               mask_first):
    bsz, t, _ = zq.shape
    c = N_HEADS * (LANES // 2)
    tb = tq * nq
    win = BAND_PAST + tq
    if mask_first:
        past_map = lambda col: (lambda b, m: (b, jnp.maximum(m - 1, 0), col))
    else:
        past_map = lambda col: (lambda b, m: (b, 0, col))
    return pl.pallas_call(
        functools.partial(_attn_kernel, tq=tq, mask_first=mask_first),
        grid=(bsz, t // tb),
        in_specs=[pl.BlockSpec((1, tb, c), lambda b, m: (b, m, q_col)),
                  pl.BlockSpec((1, BAND_PAST, c), past_map(kp_col)),
                  pl.BlockSpec((1, tb, c), lambda b, m: (b, m, kn_col)),
                  pl.BlockSpec((1, BAND_PAST, c), past_map(vp_col)),
                  pl.BlockSpec((1, tb, c), lambda b, m: (b, m, vn_col)),
                  pl.BlockSpec(evec.shape, lambda b, m: (0, 0))],
        out_specs=pl.BlockSpec((1, tb, c), lambda b, m: (b, m, 0)),
        out_shape=jax.ShapeDtypeStruct((bsz, t, c), BF16),
        scratch_shapes=[pltpu.VMEM((N_HEADS // 2, 2 * tq, win), F32),
                        pltpu.VMEM((2, nq, 2 * tq, win), F32),
                        pltpu.VMEM((2, nq, 2 * tq, win), BF16)],
        compiler_params=_params(2),
        name="band_attention",
    )(zq, k_past, k_new, v_past, v_new, evec)


def _pool_kernel(p_ref, hist_ref, w_ref, scale_ref, y_ref, ext_ref, *, pos0):
    t = pl.program_id(1)
    tm = p_ref.shape[1]
    c = p_ref.shape[2]
    pad = POOL_HIST_PAD - POOL_HIST

    @pl.when(t == 0)
    def _():
        ext_ref[0:pad, :] = jnp.zeros((pad, c), F32)
        ext_ref[pad:POOL_HIST_PAD, :] = hist_ref[0]

    @pl.when(t > 0)
    def _():
        ext_ref[0:POOL_HIST_PAD, :] = ext_ref[tm:tm + POOL_HIST_PAD, :]

    ext_ref[POOL_HIST_PAD:POOL_HIST_PAD + tm, :] = p_ref[0]
    pos = pos0 + t * tm + lax.broadcasted_iota(jnp.int32, (tm, 1), 0)
    gw = c // len(POOL_WINDOWS)
    for gi, w in enumerate(POOL_WINDOWS):
        cs = slice(gi * gw, (gi + 1) * gw)
        cur = ext_ref[POOL_HIST_PAD:POOL_HIST_PAD + tm, cs]
        win = cur
        for i in range(1, w):
            win = win + ext_ref[POOL_HIST_PAD - i:POOL_HIST_PAD - i + tm, cs]
        cnt = jnp.minimum(w, pos + 1).astype(F32)
        mgrp = (win / cnt - cur).astype(BF16)
        y = jnp.dot(mgrp, w_ref[gi], preferred_element_type=F32) * scale_ref[:, cs]
        y_ref[0, :, cs] = y.astype(y_ref.dtype)


def _sg_kernel(u_ref, v_ref, g_ref, b_ref, w_ref, bias_ref, y_ref, *vn_out):
    tm = u_ref.shape[1]
    c = u_ref.shape[2]
    span = w_ref.shape[1]
    gw = c // SG_GROUPS
    v = v_ref[0]
    mu = jnp.mean(v, axis=-1, keepdims=True)
    dlt = v - mu
    var = jnp.mean(dlt * dlt, axis=-1, keepdims=True)
    vn = dlt * lax.rsqrt(var + EPS) * g_ref[...] + b_ref[...]
    if vn_out:
        vn_out[0][0] = vn
    vb = vn.astype(BF16)
    row = lax.broadcasted_iota(jnp.int32, (span, span), 0)
    colm = lax.broadcasted_iota(jnp.int32, (span, span), 1)
    for g in range(SG_GROUPS):
        cs = slice(g * gw, (g + 1) * gw)
        wg = jnp.where(colm <= row, w_ref[g], 0.0).astype(BF16)
        for ci in range(tm // span):
            rs = slice(ci * span, (ci + 1) * span)
            s = jnp.dot(wg, vb[rs, cs], preferred_element_type=F32) + bias_ref[:, cs]
            y_ref[0, rs, cs] = (u_ref[0, rs, cs] * s).astype(y_ref.dtype)


def _row_mixers_kernel(za_ref, zg_ref, zp_ref, zu_ref, zs_ref, chist_ref, phist_ref,
                       cw_ref, cb_ref, gg_ref, gb_ref, avg_ref, pw_ref, poolw_ref, pscale_ref,
                       lng_ref, lnb_ref, sgw_ref, sgbias_ref,
                       ya_ref, cstate_ref, yc_ref, yd_ref, *rest, pos0, emit_vn):
    vn_out = rest[:1] if emit_vn else ()
    cext_ref, cacc_ref, pext_ref = rest[len(vn_out):]
    _conv_kernel(za_ref, zg_ref, chist_ref, cw_ref, cb_ref, gg_ref, gb_ref, avg_ref, pw_ref,
                 ya_ref, cstate_ref, cext_ref, cacc_ref)
    _pool_kernel(zp_ref, phist_ref, poolw_ref, pscale_ref, yc_ref, pext_ref, pos0=pos0)
    _sg_kernel(zu_ref, zs_ref, lng_ref, lnb_ref, sgw_ref, sgbias_ref, yd_ref, *vn_out)


def _row_mixers(z, conv_hist, pool_hist, conv_w, conv_b, gn_g, gn_b, avg, conv_pw, pool_w, pool_scale,
                ln_g, ln_b, sg_w, sg_bias, tm, pos0, emit_vn):
    bsz, t, _ = z.shape
    c = conv_w.shape[1]
    zcol = lambda col: pl.BlockSpec((1, tm, c), lambda b, i: (b, i, col))
    full = lambda arr: pl.BlockSpec(arr.shape, lambda b, i: (0,) * arr.ndim)
    per_seq = lambda rows: pl.BlockSpec((1, rows, c), lambda b, i: (b, 0, 0))
    tile = pl.BlockSpec((1, tm, c), lambda b, i: (b, i, 0))
    consts = (conv_w, conv_b, gn_g, gn_b, avg, conv_pw, pool_w, pool_scale, ln_g, ln_b, sg_w, sg_bias)
    out_specs = [tile, per_seq(CONV_W - 1), tile, tile]
    out_shape = [jax.ShapeDtypeStruct((bsz, t, c), BF16), jax.ShapeDtypeStruct((bsz, CONV_W - 1, c), F32),
                 jax.ShapeDtypeStruct((bsz, t, c), BF16), jax.ShapeDtypeStruct((bsz, t, c), BF16)]
    if emit_vn:
        out_specs.append(tile)
        out_shape.append(jax.ShapeDtypeStruct((bsz, t, c), F32))
    return pl.pallas_call(
        functools.partial(_row_mixers_kernel, pos0=pos0, emit_vn=emit_vn),
        grid=(bsz, t // tm),
        in_specs=[zcol(0), zcol(1), zcol(5), zcol(6), zcol(7), per_seq(CONV_W - 1), per_seq(POOL_HIST)]
                 + [full(a) for a in consts],
        out_specs=out_specs,
        out_shape=out_shape,
        scratch_shapes=[pltpu.VMEM((tm + CONV_HIST_PAD, c), F32), pltpu.VMEM((tm, c), F32),
                        pltpu.VMEM((tm + POOL_HIST_PAD, c), F32)],
        compiler_params=_params(2),
        name="row_mixers",
    )(z, z, z, z, z, conv_hist, pool_hist, *consts)


def _outproj_kernel(x_ref, ya_ref, yb_ref, yc_ref, yd_ref, g1_ref, w_ref, n2_ref, sc2_ref, sh2_ref, o_ref, h_ref):
    tm = x_ref.shape[1]
    half = tm // 2 if tm % 16 == 0 else tm
    for r in range(tm // half):
        rs = slice(r * half, (r + 1) * half)
        mix = jnp.concatenate([ya_ref[0, rs], yb_ref[0, rs], yc_ref[0, rs], yd_ref[0, rs]], axis=-1)
        mod = lambda ref: ref[0] if ref.shape[1] == 1 else ref[0, rs]
        x1 = x_ref[0, rs] + mod(g1_ref) * jnp.dot(mix, w_ref[...], preferred_element_type=F32)
        o_ref[0, rs] = x1
        h_ref[0, rs] = (_rms(x1, n2_ref[...]) * (1.0 + mod(sc2_ref)) + mod(sh2_ref)).astype(BF16)


def _out_proj(x, ys, g1, w, n2, sc2, sh2, tm):
    bsz, t, d = x.shape
    c = ys[0].shape[2]
    yspec = pl.BlockSpec((1, tm, c), lambda b, i: (b, i, 0))
    xspec = pl.BlockSpec((1, tm, d), lambda b, i: (b, i, 0))
    return pl.pallas_call(
        _outproj_kernel,
        grid=(bsz, t // tm),
        in_specs=[xspec, yspec, yspec, yspec, yspec, _mod_spec(g1, tm),
                  pl.BlockSpec(w.shape, lambda b, i: (0, 0), pipeline_mode=pl.Buffered(1)),
                  pl.BlockSpec((1, d), lambda b, i: (0, 0)), _mod_spec(sc2, tm), _mod_spec(sh2, tm)],
        out_specs=[xspec, xspec],
        out_shape=[jax.ShapeDtypeStruct((bsz, t, d), F32), jax.ShapeDtypeStruct((bsz, t, d), BF16)],
        compiler_params=_params(2),
        name="out_proj",
    )(x, *ys, g1, w, n2, sc2, sh2)


def _ffn_kernel(x_ref, h_ref, g2_ref, wg_ref, wu_ref, wd_ref, fg_ref, o_ref, acc_ref, *, final_norm):
    f = pl.program_id(2)

    @pl.when(f == 0)
    def _():
        acc_ref[...] = jnp.zeros_like(acc_ref)

    h = h_ref[0]
    gate = jnp.dot(h, wg_ref[...], preferred_element_type=F32)
    up = jnp.dot(h, wu_ref[...], preferred_element_type=F32)
    act = (gate * _sigmoid(gate) * up).astype(BF16)
    acc_ref[...] += jnp.dot(act, wd_ref[...], preferred_element_type=F32)

    @pl.when(f == pl.num_programs(2) - 1)
    def _():
        y = x_ref[0] + g2_ref[0] * acc_ref[...]
        if final_norm:
            y = _rms(y, fg_ref[...])
        o_ref[0] = y


def _ffn(x, h, g2, wg, wu, wd, fg, tm, tf, final_norm):
    bsz, t, d = x.shape
    ff = wg.shape[1]
    xspec = pl.BlockSpec((1, tm, d), lambda b, i, f: (b, i, 0))
    return pl.pallas_call(
        functools.partial(_ffn_kernel, final_norm=final_norm),
        grid=(bsz, t // tm, ff // tf),
        in_specs=[xspec, xspec, _mod_spec(g2, tm),
                  pl.BlockSpec((d, tf), lambda b, i, f: (0, f)),
                  pl.BlockSpec((d, tf), lambda b, i, f: (0, f)),
                  pl.BlockSpec((tf, d), lambda b, i, f: (f, 0)),
                  pl.BlockSpec((1, d), lambda b, i, f: (0, 0))],
        out_specs=xspec,
        out_shape=jax.ShapeDtypeStruct((bsz, t, d), F32),
        scratch_shapes=[pltpu.VMEM((tm, d), F32)],
        compiler_params=_params(3),
        name="ffn",
    )(x, h, g2, wg, wu, wd, fg)


def _largest_tile(t, cap):
    tile = min(t, cap)
    while t % tile:
        tile //= 2
    return tile


def _bias_vectors(rel_bias, tq, length=5 * LANES):
    off = tq - 1 + BAND_PAST
    assert length >= BAND_PAST + 2 * tq - 1
    head = min(length, off - REL_CLIP + 1)
    ramp_end = min(length, off + REL_CLIP + 1)
    lead = rel_bias.shape[:-1]
    parts = [jnp.broadcast_to(rel_bias[..., :1], lead + (head,)), rel_bias[..., 1:1 + ramp_end - head],
             jnp.broadcast_to(rel_bias[..., -1:], lead + (length - ramp_end,))]
    return jnp.concatenate(parts, axis=-1)


def kernel(x_prompt, x_sample, c_prompt, c_sample, cache_conv, cache_k, cache_v, cache_pool, ada_w, ada_b, norm1_g, norm2_g, w_in, conv_w, conv_b, conv_gn_g, conv_gn_b, conv_pw, rel_bias, pool_w, pool_scale, sg_ln_g, sg_ln_b, sg_w, sg_b, w_out, ffn_gate, ffn_up, ffn_down, final_g):
    depth = ada_w.shape[0]
    bp, tp, d = x_prompt.shape
    bs, ts, _ = x_sample.shape
    dc = conv_w.shape[2]
    assert tp % (CHUNK * 8) == 0 and ts <= CHUNK and ts <= SG_CHUNK and cache_k.shape[2] == BAND_PAST

    rows = bp + bs
    rows_pad = -(-rows // 8) * 8
    c_all = jnp.concatenate([c_prompt, c_sample, jnp.zeros((rows_pad - rows, d), F32)], axis=0)
    mod = _modulation(c_all, ada_w, ada_b)

    w_in_b, w_out_b = w_in.astype(BF16), w_out.astype(BF16)
    wg_b, wu_b, wd_b = ffn_gate.astype(BF16), ffn_up.astype(BF16), ffn_down.astype(BF16)
    pw_b, poolw_b = conv_pw.astype(BF16), pool_w.astype(BF16)
    gsz = dc // CONV_GROUPS
    gid = jnp.arange(dc) // gsz
    avg = jnp.where(gid[:, None] == gid[None, :], 1.0 / gsz, 0.0).astype(BF16)

    zero_conv = jnp.zeros((bp, CONV_W - 1, dc), F32)
    zero_pool = jnp.zeros((bp, POOL_HIST, dc), F32)
    evec_p, evec_s = _bias_vectors(rel_bias, CHUNK), _bias_vectors(rel_bias, ts)
    cache_k2 = cache_k.reshape(depth, bs, BAND_PAST, dc)
    cache_v2 = cache_v.reshape(depth, bs, BAND_PAST, dc)

    tm_p = _largest_tile(tp, 512)
    ns = bs * ts
    xp = x_prompt
    xs = x_sample.reshape(1, ns, d)
    outs = {k: [] for k in ("conv_p", "conv_s", "k_p", "v_p", "k_s", "v_s", "pool_p", "pool_s", "sgv_s")}
    row = lambda a: a.reshape(1, -1)

    for l in range(depth):
        mp = [m.reshape(bp, 1, d) for m in jnp.split(mod[l, :bp], 6, axis=-1)]
        ms = [jnp.repeat(m, ts, axis=0).reshape(1, ns, d) for m in jnp.split(mod[l, bp:rows], 6, axis=-1)]
        n1, n2 = row(norm1_g[l]), row(norm2_g[l])
        sgb_p = jnp.repeat(sg_b[l][:, :SG_CHUNK].T, dc // SG_GROUPS, axis=1)
        sgb_s = jnp.repeat(sg_b[l][:, :ts].T, dc // SG_GROUPS, axis=1)
        mixer_w = (conv_w[l], row(conv_b[l]), row(conv_gn_g[l]), row(conv_gn_b[l]), avg, pw_b[l],
                   poolw_b[l], row(pool_scale[l]), row(sg_ln_g[l]), row(sg_ln_b[l]))
        final = l == depth - 1

        z = _in_proj(xp, n1, mp[1], mp[0], w_in_b[l], tm_p)
        ya, conv_state, yc, yd = _row_mixers(z, zero_conv, zero_pool, *mixer_w, sg_w[l], sgb_p, tm_p, 0, False)
        yb = _attention(z, 2, z, 3, z, 4, z, 3, z, 4, evec_p[l], CHUNK, BAND_PAST // CHUNK, True)
        xp, hp = _out_proj(xp, (ya, yb, yc, yd), mp[2], w_out_b[l], n2, mp[4], mp[3], tm_p)
        xp = _ffn(xp, hp, mp[5], wg_b[l], wu_b[l], wd_b[l], row(final_g), tm_p, 512, final)
        outs["conv_p"].append(conv_state)
        outs["k_p"].append(z[:, tp - BAND_PAST:, 3 * dc:4 * dc].reshape(bp, BAND_PAST, N_HEADS, -1))
        outs["v_p"].append(z[:, tp - BAND_PAST:, 4 * dc:5 * dc].reshape(bp, BAND_PAST, N_HEADS, -1))
        outs["pool_p"].append(z[:, tp - POOL_HIST:, 5 * dc:6 * dc])

        zs = _in_proj(xs, n1, ms[1], ms[0], w_in_b[l], ns).reshape(bs, ts, -1)
        ya, conv_state, yc, yd, sgv = _row_mixers(zs, cache_conv[l], cache_pool[l], *mixer_w,
                                                  sg_w[l][:, :ts, :ts], sgb_s, ts, PAST_LEN, True)
        yb = _attention(zs, 2, zs, 3, zs, 4, cache_k2[l], 0, cache_v2[l], 0, evec_s[l], ts, 1, False)
        ys = tuple(y.reshape(1, ns, dc) for y in (ya, yb, yc, yd))
        xs, hs = _out_proj(xs, ys, ms[2], w_out_b[l], n2, ms[4], ms[3], ns)
        xs = _ffn(xs, hs, ms[5], wg_b[l], wu_b[l], wd_b[l], row(final_g), ns, 512, final)
        outs["conv_s"].append(conv_state)
        outs["k_s"].append(zs[:, :, 3 * dc:4 * dc].reshape(bs, ts, N_HEADS, -1))
        outs["v_s"].append(zs[:, :, 4 * dc:5 * dc].reshape(bs, ts, N_HEADS, -1))
        outs["pool_s"].append(jnp.concatenate([cache_pool[l], zs[:, :, 5 * dc:6 * dc]], axis=1)[:, -POOL_HIST:])
        outs["sgv_s"].append(sgv)

    st = {k: jnp.stack(v) for k, v in outs.items()}
    return (xp, xs.reshape(bs, ts, d), st["conv_p"], st["conv_s"], st["k_p"], st["v_p"], st["k_s"], st["v_s"],
            st["pool_p"], st["pool_s"], st["sgv_s"])
```

```python
import functools

import jax
import jax.numpy as jnp
from jax import lax
from jax.experimental import pallas as pl
from jax.experimental.pallas import tpu as pltpu

F32 = jnp.float32
BF16 = jnp.bfloat16

EPS = 1e-6
NEG_INF = -1e30
CHUNK = 64
BAND_PAST = 512
REL_CLIP = 128
N_HEADS = 8
CONV_W = 31
CONV_GROUPS = 8
POOL_WINDOWS = (2, 4, 8, 16)
POOL_HIST = 15
SG_CHUNK = 128
SG_GROUPS = 4
LANES = 128
SUBLANES = 8
CONV_HIST_PAD = 32
POOL_HIST_PAD = 16
CONV_ROWS = 64
FFN_TILE = 512
VMEM_LIMIT = 56 * 1024 * 1024
PAST_LEN = 2048


def _params(n_axes, vmem=VMEM_LIMIT):
    return pltpu.CompilerParams(dimension_semantics=("arbitrary",) * n_axes, vmem_limit_bytes=vmem)


def _sigmoid(x):
    return 1.0 / (1.0 + jnp.exp(-x))


def _rms(x, g):
    return x * lax.rsqrt(jnp.mean(x * x, axis=-1, keepdims=True) + EPS) * g


def _mod_kernel(c_ref, w_ref, b_ref, o_ref):
    c = c_ref[...]
    a = (c * _sigmoid(c)).astype(BF16)
    o_ref[0] = jnp.dot(a, w_ref[0].astype(BF16), preferred_element_type=F32) + b_ref[0]


def _modulation(c_all, ada_w, ada_b, tn=1536):
    depth, d, n = ada_w.shape
    rows = c_all.shape[0]
    return pl.pallas_call(
        _mod_kernel,
        grid=(depth, n // tn),
        in_specs=[pl.BlockSpec((rows, d), lambda l, j: (0, 0)),
                  pl.BlockSpec((1, d, tn), lambda l, j: (l, 0, j)),
                  pl.BlockSpec((1, 1, tn), lambda l, j: (l, 0, j))],
        out_specs=pl.BlockSpec((1, rows, tn), lambda l, j: (l, 0, j)),
        out_shape=jax.ShapeDtypeStruct((depth, rows, n), F32),
        compiler_params=_params(2),
        name="modulation",
    )(c_all, ada_w, ada_b.reshape(depth, 1, n))


def _inproj_kernel(x_ref, g_ref, sc_ref, sh_ref, w_ref, z_ref, h_ref):
    i = pl.program_id(0)

    @pl.when(i == 0)
    def _():
        h_ref[1] = jnp.zeros(h_ref.shape[1:], BF16)

    for par in range(2):
        @pl.when(i % 2 == par)
        def _():
            h = _rms(x_ref[...], g_ref[...]) * (1.0 + sc_ref[0]) + sh_ref[0]
            h_ref[par] = h.astype(BF16)
            z_ref[...] = jnp.dot(h_ref[1 - par], w_ref[...], preferred_element_type=F32)


def _slab(arr, l, single_buffer=False):
    nd = arr.ndim - 1
    mode = dict(pipeline_mode=pl.Buffered(1)) if single_buffer else {}
    return pl.BlockSpec((None,) + arr.shape[1:], lambda *_: (l,) + (0,) * nd, **mode)


def _seq_mods(mod, l):
    depth, seqs, d6 = mod.shape
    d = d6 // 6
    flat = mod.reshape(depth * seqs * 6, 1, d)

    def spec(j, tm, locate):
        return pl.BlockSpec((1, 1, d), lambda *g: ((l * seqs + locate(*g)[0]) * 6 + j, 0, 0))
    return flat, spec


def _row_mods(mod_rows, l):
    d = mod_rows.shape[2]

    def spec(j, tm, locate):
        return pl.BlockSpec((1, tm, d), lambda *g: (l * 6 + j, locate(*g)[1], 0))
    return mod_rows, spec


def _grid_bt(b, t, *_):
    return b, t


SH1, SC1, G1, SH2, SC2, G2 = range(6)


def _in_proj(x, norm_g, mods, w, l, tm):
    bsz, t, d = x.shape
    n = w.shape[2]
    n_tiles = bsz * t // tm
    tiles_per_seq = t // tm
    mod_arr, mod_spec = mods

    def locate(i):
        tile = jnp.minimum(i, n_tiles - 1)
        return tile // tiles_per_seq, tile % tiles_per_seq

    z = pl.pallas_call(
        _inproj_kernel,
        grid=(n_tiles + 1,),
        in_specs=[pl.BlockSpec((tm, d), lambda i: (jnp.minimum(i, n_tiles - 1), 0)),
                  _slab(norm_g, l), mod_spec(SC1, tm, locate), mod_spec(SH1, tm, locate),
                  _slab(w, l, single_buffer=True)],
        out_specs=pl.BlockSpec((tm, n), lambda i: (jnp.maximum(i - 1, 0), 0)),
        out_shape=jax.ShapeDtypeStruct((bsz * t, n), F32),
        scratch_shapes=[pltpu.VMEM((2, tm, d), BF16)],
        compiler_params=_params(1),
        name="in_proj",
    )(x.reshape(bsz * t, d), norm_g, mod_arr, mod_arr, w)
    return z.reshape(bsz, t, n)


def _group_mean(v, avg_ref):
    hi = v.astype(BF16)
    lo = (v - hi.astype(F32)).astype(BF16)
    a = avg_ref[...]
    return jnp.dot(hi, a, preferred_element_type=F32) + jnp.dot(lo, a, preferred_element_type=F32)


def _conv_kernel(za_ref, zg_ref, hist_ref, w_ref, b_ref, gg_ref, gb_ref, avg_ref, pw_ref,
                 y_ref, state_ref, ext_ref, acc_ref):
    t = pl.program_id(1)
    tm = za_ref.shape[1]
    c = za_ref.shape[2]
    hist = CONV_W - 1
    pad = CONV_HIST_PAD - hist

    @pl.when(t == 0)
    def _():
        ext_ref[0:pad, :] = jnp.zeros((pad, c), F32)
        ext_ref[pad:CONV_HIST_PAD, :] = hist_ref[0]

    @pl.when(t > 0)
    def _():
        ext_ref[0:CONV_HIST_PAD, :] = ext_ref[tm:tm + CONV_HIST_PAD, :]

    ext_ref[CONV_HIST_PAD:CONV_HIST_PAD + tm, :] = za_ref[0] * _sigmoid(zg_ref[0])

    rows = min(CONV_ROWS, tm)
    for r in range(tm // rows):
        base = r * rows
        acc = jnp.broadcast_to(b_ref[...], (rows, c))
        for phase in range(SUBLANES):
            offs = [o for o in range(pad, pad + CONV_W) if o % SUBLANES == phase]
            span = rows + (SUBLANES if phase else 0)
            grp = None
            for o in offs:
                lo = base + o - phase
                term = ext_ref[lo:lo + span, :] * w_ref[o - pad:o - pad + 1, :]
                grp = term if grp is None else grp + term
            acc = acc + grp[phase:phase + rows]
        acc_ref[base:base + rows, :] = acc

    y = acc_ref[...]
    mu = _group_mean(y, avg_ref)
    dlt = y - mu
    var = _group_mean(dlt * dlt, avg_ref)
    yn = dlt * lax.rsqrt(var + EPS) * gg_ref[...] + gb_ref[...]
    s = yn * _sigmoid(yn)
    y_ref[0] = jnp.dot(s.astype(BF16), pw_ref[...], preferred_element_type=F32).astype(y_ref.dtype)

    @pl.when(t == pl.num_programs(1) - 1)
    def _():
        state_ref[0] = ext_ref[tm + pad:tm + CONV_HIST_PAD, :]


def _attn_kernel(q_ref, kp_ref, kc_ref, vp_ref, vc_ref, e_ref, o_ref, bias_ref, s_ref, p_ref, *, tq, mask_first):
    first = (pl.program_id(0) == 0) & (pl.program_id(1) == 0)
    m = pl.program_id(1)
    nq = q_ref.shape[1] // tq
    past = kp_ref.shape[1]
    win = past + tq
    hd = LANES // 2
    elen = e_ref.shape[1]

    @pl.when(first)
    def _():
        for h in range(N_HEADS):
            rows = jnp.broadcast_to(e_ref[h:h + 1, :], (tq, elen))
            tz = pltpu.roll(rows, elen - (tq - 1), 1, stride=1, stride_axis=0)
            bias_ref[h // 2, (h % 2) * tq:(h % 2 + 1) * tq, :] = tz[:, :win]

    low = lax.broadcasted_iota(jnp.int32, (1, LANES), 1) < hd
    col = lax.broadcasted_iota(jnp.int32, (1, win), 1)

    def body(masked):
        for hp in range(q_ref.shape[2] // LANES):
            cs = slice(hp * LANES, (hp + 1) * LANES)
            buf = hp % 2
            q = (q_ref[0, :, cs] * (hd ** -0.5)).astype(BF16)
            qa = jnp.where(low, q, jnp.zeros_like(q))
            qb = jnp.where(low, jnp.zeros_like(q), q)
            kk = jnp.concatenate([kp_ref[0, :, cs], kc_ref[0, :, cs]], axis=0).astype(BF16)
            vv = jnp.concatenate([vp_ref[0, :, cs], vc_ref[0, :, cs]], axis=0).astype(BF16)
            for c in range(nq):
                q2 = jnp.concatenate([qa[c * tq:(c + 1) * tq], qb[c * tq:(c + 1) * tq]], axis=0)
                s = lax.dot_general(q2, kk[c * tq:c * tq + win], (((1,), (1,)), ((), ())),
                                    preferred_element_type=F32) + bias_ref[hp]
                if masked:
                    s = jnp.where(col >= past - c * tq, s, NEG_INF)
                s_ref[buf, c] = s
            inv = []
            for c in range(nq):
                s = s_ref[buf, c]
                p = jnp.exp(s - jnp.max(s, axis=-1, keepdims=True))
                inv.append(1.0 / jnp.sum(p, axis=-1, keepdims=True))
                p_ref[buf, c] = p.astype(BF16)
            for c in range(nq):
                o2 = jnp.dot(p_ref[buf, c], vv[c * tq:c * tq + win], preferred_element_type=F32) * inv[c]
                o = jnp.where(low, o2[:tq], o2[tq:])
                o_ref[0, c * tq:(c + 1) * tq, cs] = o.astype(o_ref.dtype)

    if mask_first:
        pl.when(m == 0)(functools.partial(body, True))
        pl.when(m > 0)(functools.partial(body, False))
    else:
        body(False)


def _attention(z, k_past, v_past, evec, l, tq, nq, mask_first):
    bsz, t, _ = z.shape
    c = N_HEADS * (LANES // 2)
    tb = tq * nq
    win = BAND_PAST + tq
    zcol = lambda col: pl.BlockSpec((1, tb, c), lambda b, m: (b, m, col))
    if mask_first:
        past = lambda col: pl.BlockSpec((1, BAND_PAST, c), lambda b, m: (b, jnp.maximum(m - 1, 0), col))
        past_specs = [past(3), past(4)]
    else:
        cache = pl.BlockSpec((None, 1, BAND_PAST, c), lambda b, m: (l, b, 0, 0))
        past_specs = [cache, cache]
    return pl.pallas_call(
        functools.partial(_attn_kernel, tq=tq, mask_first=mask_first),
        grid=(bsz, t // tb),
        in_specs=[zcol(2), past_specs[0], zcol(3), past_specs[1], zcol(4), _slab(evec, l)],
        out_specs=pl.BlockSpec((1, tb, c), lambda b, m: (b, m, 0)),
        out_shape=jax.ShapeDtypeStruct((bsz, t, c), BF16),
        scratch_shapes=[pltpu.VMEM((N_HEADS // 2, 2 * tq, win), F32),
                        pltpu.VMEM((2, nq, 2 * tq, win), F32),
                        pltpu.VMEM((2, nq, 2 * tq, win), BF16)],
        compiler_params=_params(2),
        name="band_attention",
    )(z, k_past, z, v_past, z, evec)


def _pool_kernel(p_ref, hist_ref, w_ref, scale_ref, y_ref, ext_ref, *, pos0):
    t = pl.program_id(1)
    tm = p_ref.shape[1]
    c = p_ref.shape[2]
    pad = POOL_HIST_PAD - POOL_HIST

    @pl.when(t == 0)
    def _():
        ext_ref[0:pad, :] = jnp.zeros((pad, c), F32)
        ext_ref[pad:POOL_HIST_PAD, :] = hist_ref[0]

    @pl.when(t > 0)
    def _():
        ext_ref[0:POOL_HIST_PAD, :] = ext_ref[tm:tm + POOL_HIST_PAD, :]

    ext_ref[POOL_HIST_PAD:POOL_HIST_PAD + tm, :] = p_ref[0]
    pos = pos0 + t * tm + lax.broadcasted_iota(jnp.int32, (tm, 1), 0)
    gw = c // len(POOL_WINDOWS)
    for gi, w in enumerate(POOL_WINDOWS):
        cs = slice(gi * gw, (gi + 1) * gw)
        cur = ext_ref[POOL_HIST_PAD:POOL_HIST_PAD + tm, cs]
        win = cur
        for i in range(1, w):
            win = win + ext_ref[POOL_HIST_PAD - i:POOL_HIST_PAD - i + tm, cs]
        cnt = jnp.minimum(w, pos + 1).astype(F32)
        mgrp = (win / cnt - cur).astype(BF16)
        y = jnp.dot(mgrp, w_ref[gi], preferred_element_type=F32) * scale_ref[:, cs]
        y_ref[0, :, cs] = y.astype(y_ref.dtype)


def _sg_kernel(u_ref, v_ref, g_ref, b_ref, w_ref, bias_ref, y_ref, *vn_out):
    tm = u_ref.shape[1]
    c = u_ref.shape[2]
    span = bias_ref.shape[0]
    gw = c // SG_GROUPS
    v = v_ref[0]
    mu = jnp.mean(v, axis=-1, keepdims=True)
    dlt = v - mu
    var = jnp.mean(dlt * dlt, axis=-1, keepdims=True)
    vn = dlt * lax.rsqrt(var + EPS) * g_ref[...] + b_ref[...]
    if vn_out:
        vn_out[0][0] = vn
    vb = vn.astype(BF16)
    row = lax.broadcasted_iota(jnp.int32, (span, span), 0)
    colm = lax.broadcasted_iota(jnp.int32, (span, span), 1)
    for g in range(SG_GROUPS):
        cs = slice(g * gw, (g + 1) * gw)
        wg = jnp.where(colm <= row, w_ref[g, :span, :span], 0.0).astype(BF16)
        for ci in range(tm // span):
            rs = slice(ci * span, (ci + 1) * span)
            s = jnp.dot(wg, vb[rs, cs], preferred_element_type=F32) + bias_ref[:, cs]
            y_ref[0, rs, cs] = (u_ref[0, rs, cs] * s).astype(y_ref.dtype)


def _row_mixers_kernel(za_ref, zg_ref, zp_ref, zu_ref, zs_ref, chist_ref, phist_ref,
                       cw_ref, cb_ref, gg_ref, gb_ref, avg_ref, pw_ref, poolw_ref, pscale_ref,
                       lng_ref, lnb_ref, sgw_ref, sgbias_ref,
                       ya_ref, cstate_ref, yc_ref, yd_ref, *rest, pos0, emit_vn):
    vn_out = rest[:1] if emit_vn else ()
    cext_ref, cacc_ref, pext_ref = rest[len(vn_out):]
    _conv_kernel(za_ref, zg_ref, chist_ref, cw_ref, cb_ref, gg_ref, gb_ref, avg_ref, pw_ref,
                 ya_ref, cstate_ref, cext_ref, cacc_ref)
    _pool_kernel(zp_ref, phist_ref, poolw_ref, pscale_ref, yc_ref, pext_ref, pos0=pos0)
    _sg_kernel(zu_ref, zs_ref, lng_ref, lnb_ref, sgw_ref, sgbias_ref, yd_ref, *vn_out)


def _row_mixers(z, conv_hist, pool_hist, hist_layer, stacked, avg, sg_bias, l, tm, pos0, emit_vn):
    bsz, t, _ = z.shape
    c = avg.shape[0]
    conv_w, conv_b, gn_g, gn_b, conv_pw, pool_w, pool_scale, ln_g, ln_b, sg_w = stacked
    zcol = lambda col: pl.BlockSpec((1, tm, c), lambda b, i: (b, i, col))
    hist = lambda rows: pl.BlockSpec((None, 1, rows, c), lambda b, i: (hist_layer, b, 0, 0))
    tile = pl.BlockSpec((1, tm, c), lambda b, i: (b, i, 0))
    span = min(tm, SG_CHUNK)
    params = [(conv_w, _slab(conv_w, l)), (conv_b, _slab(conv_b, l)), (gn_g, _slab(gn_g, l)),
              (gn_b, _slab(gn_b, l)), (avg, pl.BlockSpec(avg.shape, lambda b, i: (0, 0))),
              (conv_pw, _slab(conv_pw, l)), (pool_w, _slab(pool_w, l)), (pool_scale, _slab(pool_scale, l)),
              (ln_g, _slab(ln_g, l)), (ln_b, _slab(ln_b, l)), (sg_w, _slab(sg_w, l)),
              (sg_bias, pl.BlockSpec((None, span, c), lambda b, i: (l, 0, 0)))]
    out_specs = [tile, pl.BlockSpec((1, CONV_W - 1, c), lambda b, i: (b, 0, 0)), tile, tile]
    out_shape = [jax.ShapeDtypeStruct((bsz, t, c), BF16), jax.ShapeDtypeStruct((bsz, CONV_W - 1, c), F32),
                 jax.ShapeDtypeStruct((bsz, t, c), BF16), jax.ShapeDtypeStruct((bsz, t, c), BF16)]
    if emit_vn:
        out_specs.append(tile)
        out_shape.append(jax.ShapeDtypeStruct((bsz, t, c), F32))
    return pl.pallas_call(
        functools.partial(_row_mixers_kernel, pos0=pos0, emit_vn=emit_vn),
        grid=(bsz, t // tm),
        in_specs=[zcol(0), zcol(1), zcol(5), zcol(6), zcol(7), hist(CONV_W - 1), hist(POOL_HIST)]
                 + [spec for _, spec in params],
        out_specs=out_specs,
        out_shape=out_shape,
        scratch_shapes=[pltpu.VMEM((tm + CONV_HIST_PAD, c), F32), pltpu.VMEM((tm, c), F32),
                        pltpu.VMEM((tm + POOL_HIST_PAD, c), F32)],
        compiler_params=_params(2),
        name="row_mixers",
    )(z, z, z, z, z, conv_hist, pool_hist, *[arr for arr, _ in params])


def _outproj_kernel(x_ref, ya_ref, yb_ref, yc_ref, yd_ref, g1_ref, w_ref, n2_ref, sc2_ref, sh2_ref, o_ref, h_ref):
    tm = x_ref.shape[1]
    half = tm // 2 if tm % 16 == 0 else tm
    for r in range(tm // half):
        rs = slice(r * half, (r + 1) * half)
        mix = jnp.concatenate([ya_ref[0, rs], yb_ref[0, rs], yc_ref[0, rs], yd_ref[0, rs]], axis=-1)
        mod = lambda ref: ref[0] if ref.shape[1] == 1 else ref[0, rs]
        x1 = x_ref[0, rs] + mod(g1_ref) * jnp.dot(mix, w_ref[...], preferred_element_type=F32)
        o_ref[0, rs] = x1
        h_ref[0, rs] = (_rms(x1, n2_ref[...]) * (1.0 + mod(sc2_ref)) + mod(sh2_ref)).astype(BF16)


def _out_proj(x, ys, mods, w, norm_g, l, tm):
    bsz, t, d = x.shape
    c = ys[0].shape[2]
    mod_arr, mod_spec = mods
    yspec = pl.BlockSpec((1, tm, c), lambda b, i: (b, i, 0))
    xspec = pl.BlockSpec((1, tm, d), lambda b, i: (b, i, 0))
    return pl.pallas_call(
        _outproj_kernel,
        grid=(bsz, t // tm),
        in_specs=[xspec, yspec, yspec, yspec, yspec, mod_spec(G1, tm, _grid_bt),
                  _slab(w, l, single_buffer=True), _slab(norm_g, l),
                  mod_spec(SC2, tm, _grid_bt), mod_spec(SH2, tm, _grid_bt)],
        out_specs=[xspec, xspec],
        out_shape=[jax.ShapeDtypeStruct((bsz, t, d), F32), jax.ShapeDtypeStruct((bsz, t, d), BF16)],
        compiler_params=_params(2),
        name="out_proj",
    )(x, *ys, mod_arr, w, norm_g, mod_arr, mod_arr)


def _ffn_kernel(x_hbm, h_ref, g2_ref, wg_ref, wu_ref, wd_ref, fg_ref, o_ref, x_buf, x_sem, *, final_norm):
    b, i, f = pl.program_id(0), pl.program_id(1), pl.program_id(2)
    tm = o_ref.shape[1]
    x_copy = pltpu.make_async_copy(x_hbm.at[b, pl.ds(pl.multiple_of(i * tm, tm), tm), :], x_buf, x_sem)

    def step(first):
        h = h_ref[0]
        gate = jnp.dot(h, wg_ref[...], preferred_element_type=F32)
        up = jnp.dot(h, wu_ref[...], preferred_element_type=F32)
        act = (gate * _sigmoid(gate) * up).astype(BF16)
        part = jnp.dot(act, wd_ref[...], preferred_element_type=F32)
        if first:
            o_ref[0] = part
        else:
            o_ref[0] += part

    @pl.when(f == 0)
    def _():
        x_copy.start()
        step(True)

    pl.when(f > 0)(functools.partial(step, False))

    @pl.when(f == pl.num_programs(2) - 1)
    def _():
        x_copy.wait()
        y = x_buf[...] + g2_ref[0] * o_ref[0]
        if final_norm:
            y = _rms(y, fg_ref[...])
        o_ref[0] = y


def _ffn(x, h, mods, wg, wu, wd, fg, l, tm, final_norm):
    bsz, t, d = x.shape
    _, nf, _, tf = wg.shape
    mod_arr, mod_spec = mods
    tile = pl.BlockSpec((1, tm, d), lambda b, i, f: (b, i, 0))
    col_tile = pl.BlockSpec((None, None, d, tf), lambda b, i, f: (l, f, 0, 0))
    return pl.pallas_call(
        functools.partial(_ffn_kernel, final_norm=final_norm),
        grid=(bsz, t // tm, nf),
        in_specs=[pl.BlockSpec(memory_space=pl.ANY), tile, mod_spec(G2, tm, _grid_bt), col_tile, col_tile,
                  pl.BlockSpec((None, tf, d), lambda b, i, f: (l, f, 0)),
                  pl.BlockSpec((1, d), lambda b, i, f: (0, 0))],
        out_specs=tile,
        out_shape=jax.ShapeDtypeStruct((bsz, t, d), F32),
        scratch_shapes=[pltpu.VMEM((tm, d), F32), pltpu.SemaphoreType.DMA(())],
        compiler_params=_params(3),
        name="ffn",
    )(x, h, mod_arr, wg, wu, wd, fg)


def _largest_tile(t, cap):
    tile = min(t, cap)
    while t % tile:
        tile //= 2
    return tile


def _bias_vectors(rel_bias, tq, length=5 * LANES):
    off = tq - 1 + BAND_PAST
    assert length >= BAND_PAST + 2 * tq - 1
    head = min(length, off - REL_CLIP + 1)
    ramp_end = min(length, off + REL_CLIP + 1)
    lead = rel_bias.shape[:-1]
    parts = [jnp.broadcast_to(rel_bias[..., :1], lead + (head,)), rel_bias[..., 1:1 + ramp_end - head],
             jnp.broadcast_to(rel_bias[..., -1:], lead + (length - ramp_end,))]
    return jnp.concatenate(parts, axis=-1)


def kernel(x_prompt, x_sample, c_prompt, c_sample, cache_conv, cache_k, cache_v, cache_pool, ada_w, ada_b, norm1_g, norm2_g, w_in, conv_w, conv_b, conv_gn_g, conv_gn_b, conv_pw, rel_bias, pool_w, pool_scale, sg_ln_g, sg_ln_b, sg_w, sg_b, w_out, ffn_gate, ffn_up, ffn_down, final_g):
    depth = ada_w.shape[0]
    bp, tp, d = x_prompt.shape
    bs, ts, _ = x_sample.shape
    dc = conv_w.shape[2]
    ff = ffn_gate.shape[2]
    ns = bs * ts
    assert tp % (CHUNK * 8) == 0 and ts <= CHUNK and ts <= SG_CHUNK and cache_k.shape[2] == BAND_PAST

    rows = bp + bs
    rows_pad = -(-rows // 8) * 8
    c_all = jnp.concatenate([c_prompt, c_sample, jnp.zeros((rows_pad - rows, d), F32)], axis=0)
    mod = _modulation(c_all, ada_w, ada_b)
    mod_s = mod[:, bp:rows].reshape(depth, bs, 6, d).transpose(0, 2, 1, 3)
    mod_s = jnp.repeat(mod_s, ts, axis=2).reshape(depth * 6, ns, d)

    col_tiles = lambda w: w.astype(BF16).reshape(depth, d, ff // FFN_TILE, FFN_TILE).transpose(0, 2, 1, 3)
    w_in_b, w_out_b = w_in.astype(BF16), w_out.astype(BF16)
    wg_b, wu_b, wd_b = col_tiles(ffn_gate), col_tiles(ffn_up), ffn_down.astype(BF16)
    vec = lambda a: a.reshape(depth, 1, -1)
    norm1, norm2 = vec(norm1_g), vec(norm2_g)
    mixer_w = (conv_w, vec(conv_b), vec(conv_gn_g), vec(conv_gn_b), conv_pw.astype(BF16), pool_w.astype(BF16),
               vec(pool_scale), vec(sg_ln_g), vec(sg_ln_b), sg_w)
    gsz = dc // CONV_GROUPS
    gid = jnp.arange(dc) // gsz
    avg = jnp.where(gid[:, None] == gid[None, :], 1.0 / gsz, 0.0).astype(BF16)
    sg_bias = jnp.repeat(sg_b.transpose(0, 2, 1), dc // SG_GROUPS, axis=2)
    evec_p, evec_s = _bias_vectors(rel_bias, CHUNK), _bias_vectors(rel_bias, ts)
    zero_conv = jnp.zeros((1, bp, CONV_W - 1, dc), F32)
    zero_pool = jnp.zeros((1, bp, POOL_HIST, dc), F32)
    cache_k2 = cache_k.reshape(depth, bs, BAND_PAST, dc)
    cache_v2 = cache_v.reshape(depth, bs, BAND_PAST, dc)
    fg = final_g.reshape(1, d)

    tm_p = _largest_tile(tp, 512)
    tm_f = _largest_tile(tp, 1024)
    xp = x_prompt
    xs = x_sample.reshape(1, ns, d)
    outs = {k: [] for k in ("conv_p", "conv_s", "k_p", "v_p", "k_s", "v_s", "pool_p", "pool_s", "sgv_s")}

    for l in range(depth):
        mods_p, mods_s = _seq_mods(mod, l), _row_mods(mod_s, l)
        final = l == depth - 1

        z = _in_proj(xp, norm1, mods_p, w_in_b, l, tm_p)
        ya, conv_state, yc, yd = _row_mixers(z, zero_conv, zero_pool, 0, mixer_w, avg, sg_bias, l, tm_p, 0, False)
        yb = _attention(z, z, z, evec_p, l, CHUNK, BAND_PAST // CHUNK, True)
        xp, hp = _out_proj(xp, (ya, yb, yc, yd), mods_p, w_out_b, norm2, l, tm_p)
        xp = _ffn(xp, hp, mods_p, wg_b, wu_b, wd_b, fg, l, tm_f, final)
        outs["conv_p"].append(conv_state)
        outs["k_p"].append(z[:, tp - BAND_PAST:, 3 * dc:4 * dc].reshape(bp, BAND_PAST, N_HEADS, -1))
        outs["v_p"].append(z[:, tp - BAND_PAST:, 4 * dc:5 * dc].reshape(bp, BAND_PAST, N_HEADS, -1))
        outs["pool_p"].append(z[:, tp - POOL_HIST:, 5 * dc:6 * dc])

        zs = _in_proj(xs, norm1, mods_s, w_in_b, l, ns).reshape(bs, ts, -1)
        ya, conv_state, yc, yd, sgv = _row_mixers(zs, cache_conv, cache_pool, l, mixer_w, avg, sg_bias, l, ts,
                                                  PAST_LEN, True)
        yb = _attention(zs, cache_k2, cache_v2, evec_s, l, ts, 1, False)
        ys = tuple(y.reshape(1, ns, dc) for y in (ya, yb, yc, yd))
        xs, hs = _out_proj(xs, ys, mods_s, w_out_b, norm2, l, ns)
        xs = _ffn(xs, hs, mods_s, wg_b, wu_b, wd_b, fg, l, ns, final)
        outs["conv_s"].append(conv_state)
        outs["k_s"].append(zs[:, :, 3 * dc:4 * dc].reshape(bs, ts, N_HEADS, -1))
        outs["v_s"].append(zs[:, :, 4 * dc:5 * dc].reshape(bs, ts, N_HEADS, -1))
        outs["pool_s"].append(jnp.concatenate([cache_pool[l], zs[:, :, 5 * dc:6 * dc]], axis=1)[:, -POOL_HIST:])
        outs["sgv_s"].append(sgv)

    st = {k: jnp.stack(v) for k, v in outs.items()}
    return (xp, xs.reshape(bs, ts, d), st["conv_p"], st["conv_s"], st["k_p"], st["v_p"], st["k_s"], st["v_s"],
            st["pool_p"], st["pool_s"], st["sgv_s"])
```

```python
import functools

import jax
import jax.numpy as jnp
from jax import lax
from jax.experimental import pallas as pl
from jax.experimental.pallas import tpu as pltpu

F32 = jnp.float32
BF16 = jnp.bfloat16

EPS = 1e-6
NEG_INF = -1e30
CHUNK = 64
BAND_PAST = 512
REL_CLIP = 128
N_HEADS = 8
CONV_W = 31
CONV_GROUPS = 8
POOL_WINDOWS = (2, 4, 8, 16)
POOL_HIST = 15
SG_CHUNK = 128
SG_GROUPS = 4
LANES = 128
SUBLANES = 8
CONV_HIST_PAD = 32
POOL_HIST_PAD = 16
CONV_ROWS = 256
FFN_TILE = 512
VMEM_LIMIT = 56 * 1024 * 1024
PAST_LEN = 2048


def _params(n_axes, vmem=VMEM_LIMIT):
    return pltpu.CompilerParams(dimension_semantics=("arbitrary",) * n_axes, vmem_limit_bytes=vmem)


def _sigmoid(x):
    return 1.0 / (1.0 + jnp.exp(-x))


def _rms(x, g):
    return x * lax.rsqrt(jnp.mean(x * x, axis=-1, keepdims=True) + EPS) * g


def _mod_kernel(c_ref, w_ref, b_ref, o_ref):
    c = c_ref[...]
    a = (c * _sigmoid(c)).astype(BF16)
    o_ref[0] = jnp.dot(a, w_ref[0].astype(BF16), preferred_element_type=F32) + b_ref[0]


def _modulation(c_all, ada_w, ada_b, tn=1536):
    depth, d, n = ada_w.shape
    rows = c_all.shape[0]
    return pl.pallas_call(
        _mod_kernel,
        grid=(depth, n // tn),
        in_specs=[pl.BlockSpec((rows, d), lambda l, j: (0, 0)),
                  pl.BlockSpec((1, d, tn), lambda l, j: (l, 0, j)),
                  pl.BlockSpec((1, 1, tn), lambda l, j: (l, 0, j))],
        out_specs=pl.BlockSpec((1, rows, tn), lambda l, j: (l, 0, j)),
        out_shape=jax.ShapeDtypeStruct((depth, rows, n), F32),
        compiler_params=_params(2),
        name="modulation",
    )(c_all, ada_w, ada_b.reshape(depth, 1, n))


def _inproj_kernel(x_ref, g_ref, sc_ref, sh_ref, w_ref, z_ref, h_ref):
    i = pl.program_id(0)

    @pl.when(i == 0)
    def _():
        h_ref[1] = jnp.zeros(h_ref.shape[1:], BF16)

    for par in range(2):
        @pl.when(i % 2 == par)
        def _():
            h = _rms(x_ref[...], g_ref[...]) * (1.0 + sc_ref[0]) + sh_ref[0]
            h_ref[par] = h.astype(BF16)
            z_ref[...] = jnp.dot(h_ref[1 - par], w_ref[...], preferred_element_type=F32)


def _slab(arr, l, single_buffer=False):
    nd = arr.ndim - 1
    mode = dict(pipeline_mode=pl.Buffered(1)) if single_buffer else {}
    return pl.BlockSpec((None,) + arr.shape[1:], lambda *_: (l,) + (0,) * nd, **mode)


def _seq_mods(mod, l):
    depth, seqs, d6 = mod.shape
    d = d6 // 6
    flat = mod.reshape(depth * seqs * 6, 1, d)

    def spec(j, tm, locate):
        return pl.BlockSpec((1, 1, d), lambda *g: ((l * seqs + locate(*g)[0]) * 6 + j, 0, 0))
    return flat, spec


def _row_mods(mod_rows, l):
    d = mod_rows.shape[2]

    def spec(j, tm, locate):
        return pl.BlockSpec((1, tm, d), lambda *g: (l * 6 + j, locate(*g)[1], 0))
    return mod_rows, spec


def _grid_bt(b, t, *_):
    return b, t


SH1, SC1, G1, SH2, SC2, G2 = range(6)


def _in_proj(x, norm_g, mods, w, l, tm):
    bsz, t, d = x.shape
    n = w.shape[2]
    n_tiles = bsz * t // tm
    tiles_per_seq = t // tm
    mod_arr, mod_spec = mods

    def locate(i):
        tile = jnp.minimum(i, n_tiles - 1)
        return tile // tiles_per_seq, tile % tiles_per_seq

    z = pl.pallas_call(
        _inproj_kernel,
        grid=(n_tiles + 1,),
        in_specs=[pl.BlockSpec((tm, d), lambda i: (jnp.minimum(i, n_tiles - 1), 0)),
                  _slab(norm_g, l), mod_spec(SC1, tm, locate), mod_spec(SH1, tm, locate),
                  _slab(w, l, single_buffer=True)],
        out_specs=pl.BlockSpec((tm, n), lambda i: (jnp.maximum(i - 1, 0), 0)),
        out_shape=jax.ShapeDtypeStruct((bsz * t, n), F32),
        scratch_shapes=[pltpu.VMEM((2, tm, d), BF16)],
        compiler_params=_params(1),
        name="in_proj",
    )(x.reshape(bsz * t, d), norm_g, mod_arr, mod_arr, w)
    return z.reshape(bsz, t, n)


def _group_mean(v, avg_ref):
    hi = v.astype(BF16)
    lo = (v - hi.astype(F32)).astype(BF16)
    a = avg_ref[...]
    return jnp.dot(hi, a, preferred_element_type=F32) + jnp.dot(lo, a, preferred_element_type=F32)


def _conv_kernel(za_ref, zg_ref, hist_ref, w_ref, b_ref, gg_ref, gb_ref, avg_ref, pw_ref,
                 y_ref, state_ref, ext_ref, acc_ref):
    t = pl.program_id(1)
    tm = za_ref.shape[1]
    c = za_ref.shape[2]
    hist = CONV_W - 1
    pad = CONV_HIST_PAD - hist

    @pl.when(t == 0)
    def _():
        ext_ref[0:pad, :] = jnp.zeros((pad, c), F32)
        ext_ref[pad:CONV_HIST_PAD, :] = hist_ref[0]

    @pl.when(t > 0)
    def _():
        ext_ref[0:CONV_HIST_PAD, :] = ext_ref[tm:tm + CONV_HIST_PAD, :]

    ext_ref[CONV_HIST_PAD:CONV_HIST_PAD + tm, :] = za_ref[0] * _sigmoid(zg_ref[0])

    rows = min(CONV_ROWS, tm)
    for r in range(tm // rows):
        base = r * rows
        acc = jnp.broadcast_to(b_ref[...], (rows, c))
        for phase in range(SUBLANES):
            offs = [o for o in range(pad, pad + CONV_W) if o % SUBLANES == phase]
            span = rows + (SUBLANES if phase else 0)
            grp = None
            for o in offs:
                lo = base + o - phase
                term = ext_ref[lo:lo + span, :] * w_ref[o - pad:o - pad + 1, :]
                grp = term if grp is None else grp + term
            acc = acc + grp[phase:phase + rows]
        acc_ref[base:base + rows, :] = acc

    y = acc_ref[...]
    mu = _group_mean(y, avg_ref)
    dlt = y - mu
    var = _group_mean(dlt * dlt, avg_ref)
    yn = dlt * lax.rsqrt(var + EPS) * gg_ref[...] + gb_ref[...]
    s = yn * _sigmoid(yn)
    y_ref[0] = jnp.dot(s.astype(BF16), pw_ref[...], preferred_element_type=F32).astype(y_ref.dtype)

    @pl.when(t == pl.num_programs(1) - 1)
    def _():
        state_ref[0] = ext_ref[tm + pad:tm + CONV_HIST_PAD, :]


def _attn_kernel(q_ref, kp_ref, kc_ref, vp_ref, vc_ref, e_ref, o_ref, bias_ref, s_ref, p_ref, *, tq, mask_first):
    first = (pl.program_id(0) == 0) & (pl.program_id(1) == 0)
    m = pl.program_id(1)
    nq = q_ref.shape[1] // tq
    past = kp_ref.shape[1]
    win = past + tq
    hd = LANES // 2
    elen = e_ref.shape[1]

    @pl.when(first)
    def _():
        for h in range(N_HEADS):
            rows = jnp.broadcast_to(e_ref[h:h + 1, :], (tq, elen))
            tz = pltpu.roll(rows, elen - (tq - 1), 1, stride=1, stride_axis=0)
            bias_ref[h // 2, (h % 2) * tq:(h % 2 + 1) * tq, :] = tz[:, :win]

    low = lax.broadcasted_iota(jnp.int32, (1, LANES), 1) < hd
    col = lax.broadcasted_iota(jnp.int32, (1, win), 1)

    def body(masked):
        for hp in range(q_ref.shape[2] // LANES):
            cs = slice(hp * LANES, (hp + 1) * LANES)
            buf = hp % 2
            q = (q_ref[0, :, cs] * (hd ** -0.5)).astype(BF16)
            qa = jnp.where(low, q, jnp.zeros_like(q))
            qb = jnp.where(low, jnp.zeros_like(q), q)
            kk = jnp.concatenate([kp_ref[0, :, cs], kc_ref[0, :, cs]], axis=0).astype(BF16)
            vv = jnp.concatenate([vp_ref[0, :, cs], vc_ref[0, :, cs]], axis=0).astype(BF16)
            for c in range(nq):
                q2 = jnp.concatenate([qa[c * tq:(c + 1) * tq], qb[c * tq:(c + 1) * tq]], axis=0)
                s = lax.dot_general(q2, kk[c * tq:c * tq + win], (((1,), (1,)), ((), ())),
                                    preferred_element_type=F32) + bias_ref[hp]
                if masked:
                    s = jnp.where(col >= past - c * tq, s, NEG_INF)
                s_ref[buf, c] = s
            inv = []
            for c in range(nq):
                s = s_ref[buf, c]
                p = jnp.exp(s - jnp.max(s, axis=-1, keepdims=True))
                inv.append(1.0 / jnp.sum(p, axis=-1, keepdims=True))
                p_ref[buf, c] = p.astype(BF16)
            for c in range(nq):
                o2 = jnp.dot(p_ref[buf, c], vv[c * tq:c * tq + win], preferred_element_type=F32) * inv[c]
                o = jnp.where(low, o2[:tq], o2[tq:])
                o_ref[0, c * tq:(c + 1) * tq, cs] = o.astype(o_ref.dtype)

    if mask_first:
        pl.when(m == 0)(functools.partial(body, True))
        pl.when(m > 0)(functools.partial(body, False))
    else:
        body(False)


def _attention(z, k_past, v_past, evec, l, tq, nq, mask_first):
    bsz, t, _ = z.shape
    c = N_HEADS * (LANES // 2)
    tb = tq * nq
    win = BAND_PAST + tq
    zcol = lambda col: pl.BlockSpec((1, tb, c), lambda b, m: (b, m, col))
    if mask_first:
        past = lambda col: pl.BlockSpec((1, BAND_PAST, c), lambda b, m: (b, jnp.maximum(m - 1, 0), col))
        past_specs = [past(3), past(4)]
    else:
        cache = pl.BlockSpec((None, 1, BAND_PAST, c), lambda b, m: (l, b, 0, 0))
        past_specs = [cache, cache]
    return pl.pallas_call(
        functools.partial(_attn_kernel, tq=tq, mask_first=mask_first),
        grid=(bsz, t // tb),
        in_specs=[zcol(2), past_specs[0], zcol(3), past_specs[1], zcol(4), _slab(evec, l)],
        out_specs=pl.BlockSpec((1, tb, c), lambda b, m: (b, m, 0)),
        out_shape=jax.ShapeDtypeStruct((bsz, t, c), BF16),
        scratch_shapes=[pltpu.VMEM((N_HEADS // 2, 2 * tq, win), F32),
                        pltpu.VMEM((2, nq, 2 * tq, win), F32),
                        pltpu.VMEM((2, nq, 2 * tq, win), BF16)],
        compiler_params=_params(2),
        name="band_attention",
    )(z, k_past, z, v_past, z, evec)


def _pool_kernel(p_ref, hist_ref, w_ref, scale_ref, y_ref, ext_ref, *, pos0):
    t = pl.program_id(1)
    tm = p_ref.shape[1]
    c = p_ref.shape[2]
    pad = POOL_HIST_PAD - POOL_HIST

    @pl.when(t == 0)
    def _():
        ext_ref[0:pad, :] = jnp.zeros((pad, c), F32)
        ext_ref[pad:POOL_HIST_PAD, :] = hist_ref[0]

    @pl.when(t > 0)
    def _():
        ext_ref[0:POOL_HIST_PAD, :] = ext_ref[tm:tm + POOL_HIST_PAD, :]

    ext_ref[POOL_HIST_PAD:POOL_HIST_PAD + tm, :] = p_ref[0]
    pos = pos0 + t * tm + lax.broadcasted_iota(jnp.int32, (tm, 1), 0)
    gw = c // len(POOL_WINDOWS)
    for gi, w in enumerate(POOL_WINDOWS):
        cs = slice(gi * gw, (gi + 1) * gw)
        run = ext_ref[:, cs]
        shift = 1
        while shift < w:
            run = run + pltpu.roll(run, shift, 0)
            shift *= 2
        cur = ext_ref[POOL_HIST_PAD:POOL_HIST_PAD + tm, cs]
        win = run[POOL_HIST_PAD:POOL_HIST_PAD + tm]
        cnt = jnp.minimum(w, pos + 1).astype(F32)
        mgrp = (win / cnt - cur).astype(BF16)
        y = jnp.dot(mgrp, w_ref[gi], preferred_element_type=F32) * scale_ref[:, cs]
        y_ref[0, :, cs] = y.astype(y_ref.dtype)


def _sg_kernel(u_ref, v_ref, g_ref, b_ref, w_ref, bias_ref, y_ref, *vn_out):
    tm = u_ref.shape[1]
    c = u_ref.shape[2]
    span = bias_ref.shape[0]
    gw = c // SG_GROUPS
    v = v_ref[0]
    mu = jnp.mean(v, axis=-1, keepdims=True)
    dlt = v - mu
    var = jnp.mean(dlt * dlt, axis=-1, keepdims=True)
    vn = dlt * lax.rsqrt(var + EPS) * g_ref[...] + b_ref[...]
    if vn_out:
        vn_out[0][0] = vn
    vb = vn.astype(BF16)
    row = lax.broadcasted_iota(jnp.int32, (span, span), 0)
    colm = lax.broadcasted_iota(jnp.int32, (span, span), 1)
    for g in range(SG_GROUPS):
        cs = slice(g * gw, (g + 1) * gw)
        wg = jnp.where(colm <= row, w_ref[g, :span, :span], 0.0).astype(BF16)
        for ci in range(tm // span):
            rs = slice(ci * span, (ci + 1) * span)
            s = jnp.dot(wg, vb[rs, cs], preferred_element_type=F32) + bias_ref[:, cs]
            y_ref[0, rs, cs] = (u_ref[0, rs, cs] * s).astype(y_ref.dtype)


def _row_mixers_kernel(za_ref, zg_ref, zp_ref, zu_ref, zs_ref, chist_ref, phist_ref,
                       cw_ref, cb_ref, gg_ref, gb_ref, avg_ref, pw_ref, poolw_ref, pscale_ref,
                       lng_ref, lnb_ref, sgw_ref, sgbias_ref,
                       ya_ref, cstate_ref, yc_ref, yd_ref, *rest, pos0, emit_vn):
    vn_out = rest[:1] if emit_vn else ()
    cext_ref, cacc_ref, pext_ref = rest[len(vn_out):]
    _conv_kernel(za_ref, zg_ref, chist_ref, cw_ref, cb_ref, gg_ref, gb_ref, avg_ref, pw_ref,
                 ya_ref, cstate_ref, cext_ref, cacc_ref)
    _pool_kernel(zp_ref, phist_ref, poolw_ref, pscale_ref, yc_ref, pext_ref, pos0=pos0)
    _sg_kernel(zu_ref, zs_ref, lng_ref, lnb_ref, sgw_ref, sgbias_ref, yd_ref, *vn_out)


def _row_mixers(z, conv_hist, pool_hist, hist_layer, stacked, avg, sg_bias, l, tm, pos0, emit_vn):
    bsz, t, _ = z.shape
    c = avg.shape[0]
    conv_w, conv_b, gn_g, gn_b, conv_pw, pool_w, pool_scale, ln_g, ln_b, sg_w = stacked
    zcol = lambda col: pl.BlockSpec((1, tm, c), lambda b, i: (b, i, col))
    hist = lambda rows: pl.BlockSpec((None, 1, rows, c), lambda b, i: (hist_layer, b, 0, 0))
    tile = pl.BlockSpec((1, tm, c), lambda b, i: (b, i, 0))
    span = min(tm, SG_CHUNK)
    params = [(conv_w, _slab(conv_w, l)), (conv_b, _slab(conv_b, l)), (gn_g, _slab(gn_g, l)),
              (gn_b, _slab(gn_b, l)), (avg, pl.BlockSpec(avg.shape, lambda b, i: (0, 0))),
              (conv_pw, _slab(conv_pw, l)), (pool_w, _slab(pool_w, l)), (pool_scale, _slab(pool_scale, l)),
              (ln_g, _slab(ln_g, l)), (ln_b, _slab(ln_b, l)), (sg_w, _slab(sg_w, l)),
              (sg_bias, pl.BlockSpec((None, span, c), lambda b, i: (l, 0, 0)))]
    out_specs = [tile, pl.BlockSpec((1, CONV_W - 1, c), lambda b, i: (b, 0, 0)), tile, tile]
    out_shape = [jax.ShapeDtypeStruct((bsz, t, c), BF16), jax.ShapeDtypeStruct((bsz, CONV_W - 1, c), F32),
                 jax.ShapeDtypeStruct((bsz, t, c), BF16), jax.ShapeDtypeStruct((bsz, t, c), BF16)]
    if emit_vn:
        out_specs.append(tile)
        out_shape.append(jax.ShapeDtypeStruct((bsz, t, c), F32))
    return pl.pallas_call(
        functools.partial(_row_mixers_kernel, pos0=pos0, emit_vn=emit_vn),
        grid=(bsz, t // tm),
        in_specs=[zcol(0), zcol(1), zcol(5), zcol(6), zcol(7), hist(CONV_W - 1), hist(POOL_HIST)]
                 + [spec for _, spec in params],
        out_specs=out_specs,
        out_shape=out_shape,
        scratch_shapes=[pltpu.VMEM((tm + CONV_HIST_PAD, c), F32), pltpu.VMEM((tm, c), F32),
                        pltpu.VMEM((tm + POOL_HIST_PAD, c), F32)],
        compiler_params=_params(2),
        name="row_mixers",
    )(z, z, z, z, z, conv_hist, pool_hist, *[arr for arr, _ in params])


def _outproj_kernel(x_ref, ya_ref, yb_ref, yc_ref, yd_ref, g1_ref, w_ref, n2_ref, sc2_ref, sh2_ref, o_ref, h_ref):
    tm = x_ref.shape[1]
    half = tm // 2 if tm % 16 == 0 else tm
    for r in range(tm // half):
        rs = slice(r * half, (r + 1) * half)
        mix = jnp.concatenate([ya_ref[0, rs], yb_ref[0, rs], yc_ref[0, rs], yd_ref[0, rs]], axis=-1)
        mod = lambda ref: ref[0] if ref.shape[1] == 1 else ref[0, rs]
        x1 = x_ref[0, rs] + mod(g1_ref) * jnp.dot(mix, w_ref[...], preferred_element_type=F32)
        o_ref[0, rs] = x1
        h_ref[0, rs] = (_rms(x1, n2_ref[...]) * (1.0 + mod(sc2_ref)) + mod(sh2_ref)).astype(BF16)


def _out_proj(x, ys, mods, w, norm_g, l, tm):
    bsz, t, d = x.shape
    c = ys[0].shape[2]
    mod_arr, mod_spec = mods
    yspec = pl.BlockSpec((1, tm, c), lambda b, i: (b, i, 0))
    xspec = pl.BlockSpec((1, tm, d), lambda b, i: (b, i, 0))
    return pl.pallas_call(
        _outproj_kernel,
        grid=(bsz, t // tm),
        in_specs=[xspec, yspec, yspec, yspec, yspec, mod_spec(G1, tm, _grid_bt),
                  _slab(w, l, single_buffer=True), _slab(norm_g, l),
                  mod_spec(SC2, tm, _grid_bt), mod_spec(SH2, tm, _grid_bt)],
        out_specs=[xspec, xspec],
        out_shape=[jax.ShapeDtypeStruct((bsz, t, d), F32), jax.ShapeDtypeStruct((bsz, t, d), BF16)],
        compiler_params=_params(2),
        name="out_proj",
    )(x, *ys, mod_arr, w, norm_g, mod_arr, mod_arr)


def _ffn_kernel(x_hbm, h_ref, g2_ref, wg_ref, wu_ref, wd_ref, fg_ref, o_ref, x_buf, x_sem, *, final_norm):
    b, i, f = pl.program_id(0), pl.program_id(1), pl.program_id(2)
    tm = o_ref.shape[1]
    x_copy = pltpu.make_async_copy(x_hbm.at[b, pl.ds(pl.multiple_of(i * tm, tm), tm), :], x_buf, x_sem)

    def step(first):
        h = h_ref[0]
        gate = jnp.dot(h, wg_ref[...], preferred_element_type=F32)
        up = jnp.dot(h, wu_ref[...], preferred_element_type=F32)
        act = (gate * _sigmoid(gate) * up).astype(BF16)
        part = jnp.dot(act, wd_ref[...], preferred_element_type=F32)
        if first:
            o_ref[0] = part
        else:
            o_ref[0] += part

    @pl.when(f == 0)
    def _():
        x_copy.start()
        step(True)

    pl.when(f > 0)(functools.partial(step, False))

    @pl.when(f == pl.num_programs(2) - 1)
    def _():
        x_copy.wait()
        y = x_buf[...] + g2_ref[0] * o_ref[0]
        if final_norm:
            y = _rms(y, fg_ref[...])
        o_ref[0] = y


def _ffn(x, h, mods, wg, wu, wd, fg, l, tm, final_norm):
    bsz, t, d = x.shape
    _, nf, _, tf = wg.shape
    mod_arr, mod_spec = mods
    tile = pl.BlockSpec((1, tm, d), lambda b, i, f: (b, i, 0))
    col_tile = pl.BlockSpec((None, None, d, tf), lambda b, i, f: (l, f, 0, 0))
    return pl.pallas_call(
        functools.partial(_ffn_kernel, final_norm=final_norm),
        grid=(bsz, t // tm, nf),
        in_specs=[pl.BlockSpec(memory_space=pl.ANY), tile, mod_spec(G2, tm, _grid_bt), col_tile, col_tile,
                  pl.BlockSpec((None, tf, d), lambda b, i, f: (l, f, 0)),
                  pl.BlockSpec((1, d), lambda b, i, f: (0, 0))],
        out_specs=tile,
        out_shape=jax.ShapeDtypeStruct((bsz, t, d), F32),
        scratch_shapes=[pltpu.VMEM((tm, d), F32), pltpu.SemaphoreType.DMA(())],
        compiler_params=_params(3),
        name="ffn",
    )(x, h, mod_arr, wg, wu, wd, fg)


def _col_tiles_kernel(a_ref, b_ref, oa_ref, ob_ref):
    oa_ref[...] = a_ref[...].astype(oa_ref.dtype)
    ob_ref[...] = b_ref[...].astype(ob_ref.dtype)


def _col_tiles(wa, wb, tf):
    depth, d, ff = wa.shape
    src = pl.BlockSpec((None, d, tf), lambda l, f: (l, 0, f))
    dst = pl.BlockSpec((None, None, d, tf), lambda l, f: (l, f, 0, 0))
    shape = jax.ShapeDtypeStruct((depth, ff // tf, d, tf), BF16)
    return pl.pallas_call(
        _col_tiles_kernel,
        grid=(depth, ff // tf),
        in_specs=[src, src],
        out_specs=[dst, dst],
        out_shape=[shape, shape],
        compiler_params=_params(2),
        name="ffn_weight_tiles",
    )(wa, wb)


def _largest_tile(t, cap):
    tile = min(t, cap)
    while t % tile:
        tile //= 2
    return tile


def _bias_vectors(rel_bias, tq, length=5 * LANES):
    off = tq - 1 + BAND_PAST
    assert length >= BAND_PAST + 2 * tq - 1
    head = min(length, off - REL_CLIP + 1)
    ramp_end = min(length, off + REL_CLIP + 1)
    lead = rel_bias.shape[:-1]
    parts = [jnp.broadcast_to(rel_bias[..., :1], lead + (head,)), rel_bias[..., 1:1 + ramp_end - head],
             jnp.broadcast_to(rel_bias[..., -1:], lead + (length - ramp_end,))]
    return jnp.concatenate(parts, axis=-1)


def kernel(x_prompt, x_sample, c_prompt, c_sample, cache_conv, cache_k, cache_v, cache_pool, ada_w, ada_b, norm1_g, norm2_g, w_in, conv_w, conv_b, conv_gn_g, conv_gn_b, conv_pw, rel_bias, pool_w, pool_scale, sg_ln_g, sg_ln_b, sg_w, sg_b, w_out, ffn_gate, ffn_up, ffn_down, final_g):
    depth = ada_w.shape[0]
    bp, tp, d = x_prompt.shape
    bs, ts, _ = x_sample.shape
    dc = conv_w.shape[2]
    ns = bs * ts
    assert tp % (CHUNK * 8) == 0 and ts <= CHUNK and ts <= SG_CHUNK and cache_k.shape[2] == BAND_PAST

    rows = bp + bs
    rows_pad = -(-rows // 8) * 8
    c_all = jnp.concatenate([c_prompt, c_sample, jnp.zeros((rows_pad - rows, d), F32)], axis=0)
    mod = _modulation(c_all, ada_w, ada_b)
    mod_s = mod[:, bp:rows].reshape(depth, bs, 6, d).transpose(0, 2, 1, 3)
    mod_s = jnp.repeat(mod_s, ts, axis=2).reshape(depth * 6, ns, d)

    w_in_b, w_out_b, wd_b = w_in.astype(BF16), w_out.astype(BF16), ffn_down.astype(BF16)
    wg_b, wu_b = _col_tiles(ffn_gate, ffn_up, FFN_TILE)
    vec = lambda a: a.reshape(depth, 1, -1)
    norm1, norm2 = vec(norm1_g), vec(norm2_g)
    mixer_w = (conv_w, vec(conv_b), vec(conv_gn_g), vec(conv_gn_b), conv_pw.astype(BF16), pool_w.astype(BF16),
               vec(pool_scale), vec(sg_ln_g), vec(sg_ln_b), sg_w)
    gsz = dc // CONV_GROUPS
    gid = jnp.arange(dc) // gsz
    avg = jnp.where(gid[:, None] == gid[None, :], 1.0 / gsz, 0.0).astype(BF16)
    sg_bias = jnp.repeat(sg_b.transpose(0, 2, 1), dc // SG_GROUPS, axis=2)
    evec_p, evec_s = _bias_vectors(rel_bias, CHUNK), _bias_vectors(rel_bias, ts)
    zero_conv = jnp.zeros((1, bp, CONV_W - 1, dc), F32)
    zero_pool = jnp.zeros((1, bp, POOL_HIST, dc), F32)
    cache_k2 = cache_k.reshape(depth, bs, BAND_PAST, dc)
    cache_v2 = cache_v.reshape(depth, bs, BAND_PAST, dc)
    fg = final_g.reshape(1, d)

    tm_p = _largest_tile(tp, 512)
    tm_f = _largest_tile(tp, 1024)
    xp = x_prompt
    xs = x_sample.reshape(1, ns, d)
    outs = {k: [] for k in ("conv_p", "conv_s", "k_p", "v_p", "k_s", "v_s", "pool_p", "pool_s", "sgv_s")}

    for l in range(depth):
        mods_p, mods_s = _seq_mods(mod, l), _row_mods(mod_s, l)
        final = l == depth - 1

        z = _in_proj(xp, norm1, mods_p, w_in_b, l, tm_p)
        ya, conv_state, yc, yd = _row_mixers(z, zero_conv, zero_pool, 0, mixer_w, avg, sg_bias, l, tm_p, 0, False)
        yb = _attention(z, z, z, evec_p, l, CHUNK, BAND_PAST // CHUNK, True)
        xp, hp = _out_proj(xp, (ya, yb, yc, yd), mods_p, w_out_b, norm2, l, tm_p)
        xp = _ffn(xp, hp, mods_p, wg_b, wu_b, wd_b, fg, l, tm_f, final)
        outs["conv_p"].append(conv_state)
        outs["k_p"].append(z[:, tp - BAND_PAST:, 3 * dc:4 * dc].reshape(bp, BAND_PAST, N_HEADS, -1))
        outs["v_p"].append(z[:, tp - BAND_PAST:, 4 * dc:5 * dc].reshape(bp, BAND_PAST, N_HEADS, -1))
        outs["pool_p"].append(z[:, tp - POOL_HIST:, 5 * dc:6 * dc])

        zs = _in_proj(xs, norm1, mods_s, w_in_b, l, ns).reshape(bs, ts, -1)
        ya, conv_state, yc, yd, sgv = _row_mixers(zs, cache_conv, cache_pool, l, mixer_w, avg, sg_bias, l, ts,
                                                  PAST_LEN, True)
        yb = _attention(zs, cache_k2, cache_v2, evec_s, l, ts, 1, False)
        ys = tuple(y.reshape(1, ns, dc) for y in (ya, yb, yc, yd))
        xs, hs = _out_proj(xs, ys, mods_s, w_out_b, norm2, l, ns)
        xs = _ffn(xs, hs, mods_s, wg_b, wu_b, wd_b, fg, l, ns, final)
        outs["conv_s"].append(conv_state)
        outs["k_s"].append(zs[:, :, 3 * dc:4 * dc].reshape(bs, ts, N_HEADS, -1))
        outs["v_s"].append(zs[:, :, 4 * dc:5 * dc].reshape(bs, ts, N_HEADS, -1))
        outs["pool_s"].append(jnp.concatenate([cache_pool[l], zs[:, :, 5 * dc:6 * dc]], axis=1)[:, -POOL_HIST:])
        outs["sgv_s"].append(sgv)

    st = {k: jnp.stack(v) for k, v in outs.items()}
    return (xp, xs.reshape(bs, ts, d), st["conv_p"], st["conv_s"], st["k_p"], st["v_p"], st["k_s"], st["v_s"],
            st["pool_p"], st["pool_s"], st["sgv_s"])
```

```python
import functools

import jax
import jax.numpy as jnp
from jax import lax
from jax.experimental import pallas as pl
from jax.experimental.pallas import tpu as pltpu

F32 = jnp.float32
BF16 = jnp.bfloat16

EPS = 1e-6
NEG_INF = -1e30
CHUNK = 64
BAND_PAST = 512
REL_CLIP = 128
N_HEADS = 8
CONV_W = 31
CONV_GROUPS = 8
POOL_WINDOWS = (2, 4, 8, 16)
POOL_HIST = 15
SG_CHUNK = 128
SG_GROUPS = 4
LANES = 128
SUBLANES = 8
CONV_HIST_PAD = 32
POOL_HIST_PAD = 16
CONV_ROWS = 256
FFN_TILE = 512
VMEM_LIMIT = 56 * 1024 * 1024
PAST_LEN = 2048


def _params(n_axes, vmem=VMEM_LIMIT):
    return pltpu.CompilerParams(dimension_semantics=("arbitrary",) * n_axes, vmem_limit_bytes=vmem)


def _sigmoid(x):
    return 1.0 / (1.0 + jnp.exp(-x))


def _rms(x, g):
    return x * lax.rsqrt(jnp.mean(x * x, axis=-1, keepdims=True) + EPS) * g


def _mod_kernel(c_ref, w_ref, b_ref, o_ref):
    c = c_ref[...]
    a = (c * _sigmoid(c)).astype(BF16)
    o_ref[0] = jnp.dot(a, w_ref[0].astype(BF16), preferred_element_type=F32) + b_ref[0]


def _modulation(c_all, ada_w, ada_b, tn=1536):
    depth, d, n = ada_w.shape
    rows = c_all.shape[0]
    return pl.pallas_call(
        _mod_kernel,
        grid=(depth, n // tn),
        in_specs=[pl.BlockSpec((rows, d), lambda l, j: (0, 0)),
                  pl.BlockSpec((1, d, tn), lambda l, j: (l, 0, j)),
                  pl.BlockSpec((1, 1, tn), lambda l, j: (l, 0, j))],
        out_specs=pl.BlockSpec((1, rows, tn), lambda l, j: (l, 0, j)),
        out_shape=jax.ShapeDtypeStruct((depth, rows, n), F32),
        compiler_params=_params(2),
        name="modulation",
    )(c_all, ada_w, ada_b.reshape(depth, 1, n))


def _inproj_kernel(x_ref, g_ref, sc_ref, sh_ref, w_ref, z_ref):
    tm = x_ref.shape[0]
    half = tm // 2 if tm % (2 * SUBLANES) == 0 else tm
    for r in range(tm // half):
        rs = slice(r * half, (r + 1) * half)
        mod = lambda ref: ref[0] if ref.shape[1] == 1 else ref[0, rs]
        h = _rms(x_ref[rs], g_ref[...]) * (1.0 + mod(sc_ref)) + mod(sh_ref)
        z_ref[rs] = jnp.dot(h.astype(BF16), w_ref[...], preferred_element_type=F32)


def _slab(arr, l, single_buffer=False):
    nd = arr.ndim - 1
    mode = dict(pipeline_mode=pl.Buffered(1)) if single_buffer else {}
    return pl.BlockSpec((None,) + arr.shape[1:], lambda *_: (l,) + (0,) * nd, **mode)


def _seq_mods(mod, l):
    depth, seqs, d6 = mod.shape
    d = d6 // 6
    flat = mod.reshape(depth * seqs * 6, 1, d)

    def spec(j, tm, locate):
        return pl.BlockSpec((1, 1, d), lambda *g: ((l * seqs + locate(*g)[0]) * 6 + j, 0, 0))
    return flat, spec


def _row_mods(mod_rows, l):
    d = mod_rows.shape[2]

    def spec(j, tm, locate):
        return pl.BlockSpec((1, tm, d), lambda *g: (l * 6 + j, locate(*g)[1], 0))
    return mod_rows, spec


def _grid_bt(b, t, *_):
    return b, t


SH1, SC1, G1, SH2, SC2, G2 = range(6)


def _in_proj(x, norm_g, mods, w, l, tm):
    bsz, t, d = x.shape
    n = w.shape[2]
    n_tiles = bsz * t // tm
    tiles_per_seq = t // tm
    mod_arr, mod_spec = mods
    locate = lambda i: (i // tiles_per_seq, i % tiles_per_seq)
    z = pl.pallas_call(
        _inproj_kernel,
        grid=(n_tiles,),
        in_specs=[pl.BlockSpec((tm, d), lambda i: (i, 0)),
                  _slab(norm_g, l), mod_spec(SC1, tm, locate), mod_spec(SH1, tm, locate),
                  _slab(w, l, single_buffer=True)],
        out_specs=pl.BlockSpec((tm, n), lambda i: (i, 0)),
        out_shape=jax.ShapeDtypeStruct((bsz * t, n), F32),
        compiler_params=_params(1),
        name="in_proj",
    )(x.reshape(bsz * t, d), norm_g, mod_arr, mod_arr, w)
    return z.reshape(bsz, t, n)


def _group_mean(v, avg_ref):
    hi = v.astype(BF16)
    lo = (v - hi.astype(F32)).astype(BF16)
    a = avg_ref[...]
    return jnp.dot(hi, a, preferred_element_type=F32) + jnp.dot(lo, a, preferred_element_type=F32)


def _conv_kernel(za_ref, zg_ref, hist_ref, w_ref, b_ref, gg_ref, gb_ref, avg_ref, pw_ref,
                 y_ref, state_ref, ext_ref, acc_ref):
    t = pl.program_id(1)
    tm = za_ref.shape[1]
    c = za_ref.shape[2]
    hist = CONV_W - 1
    pad = CONV_HIST_PAD - hist

    @pl.when(t == 0)
    def _():
        ext_ref[0:pad, :] = jnp.zeros((pad, c), F32)
        ext_ref[pad:CONV_HIST_PAD, :] = hist_ref[0]

    @pl.when(t > 0)
    def _():
        ext_ref[0:CONV_HIST_PAD, :] = ext_ref[tm:tm + CONV_HIST_PAD, :]

    ext_ref[CONV_HIST_PAD:CONV_HIST_PAD + tm, :] = za_ref[0] * _sigmoid(zg_ref[0])

    rows = min(CONV_ROWS, tm)
    for r in range(tm // rows):
        base = r * rows
        acc = jnp.broadcast_to(b_ref[...], (rows, c))
        for phase in range(SUBLANES):
            offs = [o for o in range(pad, pad + CONV_W) if o % SUBLANES == phase]
            span = rows + (SUBLANES if phase else 0)
            grp = None
            for o in offs:
                lo = base + o - phase
                term = ext_ref[lo:lo + span, :] * w_ref[o - pad:o - pad + 1, :]
                grp = term if grp is None else grp + term
            acc = acc + grp[phase:phase + rows]
        acc_ref[base:base + rows, :] = acc

    y = acc_ref[...]
    mu = _group_mean(y, avg_ref)
    dlt = y - mu
    var = _group_mean(dlt * dlt, avg_ref)
    yn = dlt * lax.rsqrt(var + EPS) * gg_ref[...] + gb_ref[...]
    s = yn * _sigmoid(yn)
    y_ref[0] = jnp.dot(s.astype(BF16), pw_ref[...], preferred_element_type=F32).astype(y_ref.dtype)

    @pl.when(t == pl.num_programs(1) - 1)
    def _():
        state_ref[0] = ext_ref[tm + pad:tm + CONV_HIST_PAD, :]


def _attn_kernel(q_ref, kp_ref, kc_ref, vp_ref, vc_ref, e_ref, o_ref, bias_ref, s_ref, p_ref, *, tq, mask_first):
    first = (pl.program_id(0) == 0) & (pl.program_id(1) == 0)
    m = pl.program_id(1)
    nq = q_ref.shape[1] // tq
    past = kp_ref.shape[1]
    win = past + tq
    hd = LANES // 2
    elen = e_ref.shape[1]

    @pl.when(first)
    def _():
        for h in range(N_HEADS):
            rows = jnp.broadcast_to(e_ref[h:h + 1, :], (tq, elen))
            tz = pltpu.roll(rows, elen - (tq - 1), 1, stride=1, stride_axis=0)
            bias_ref[h // 2, (h % 2) * tq:(h % 2 + 1) * tq, :] = tz[:, :win]

    low = lax.broadcasted_iota(jnp.int32, (1, LANES), 1) < hd
    col = lax.broadcasted_iota(jnp.int32, (1, win), 1)

    def body(masked):
        for hp in range(q_ref.shape[2] // LANES):
            cs = slice(hp * LANES, (hp + 1) * LANES)
            buf = hp % 2
            q = (q_ref[0, :, cs] * (hd ** -0.5)).astype(BF16)
            qa = jnp.where(low, q, jnp.zeros_like(q))
            qb = jnp.where(low, jnp.zeros_like(q), q)
            kk = jnp.concatenate([kp_ref[0, :, cs], kc_ref[0, :, cs]], axis=0).astype(BF16)
            vv = jnp.concatenate([vp_ref[0, :, cs], vc_ref[0, :, cs]], axis=0).astype(BF16)
            for c in range(nq):
                q2 = jnp.concatenate([qa[c * tq:(c + 1) * tq], qb[c * tq:(c + 1) * tq]], axis=0)
                s = lax.dot_general(q2, kk[c * tq:c * tq + win], (((1,), (1,)), ((), ())),
                                    preferred_element_type=F32) + bias_ref[hp]
                if masked:
                    s = jnp.where(col >= past - c * tq, s, NEG_INF)
                s_ref[buf, c] = s
            inv = []
            for c in range(nq):
                s = s_ref[buf, c]
                p = jnp.exp(s - jnp.max(s, axis=-1, keepdims=True))
                inv.append(1.0 / jnp.sum(p, axis=-1, keepdims=True))
                p_ref[buf, c] = p.astype(BF16)
            for c in range(nq):
                o2 = jnp.dot(p_ref[buf, c], vv[c * tq:c * tq + win], preferred_element_type=F32) * inv[c]
                o = jnp.where(low, o2[:tq], o2[tq:])
                o_ref[0, c * tq:(c + 1) * tq, cs] = o.astype(o_ref.dtype)

    if mask_first:
        pl.when(m == 0)(functools.partial(body, True))
        pl.when(m > 0)(functools.partial(body, False))
    else:
        body(False)


def _attention(z, k_past, v_past, evec, l, tq, nq, mask_first):
    bsz, t, _ = z.shape
    c = N_HEADS * (LANES // 2)
    tb = tq * nq
    win = BAND_PAST + tq
    zcol = lambda col: pl.BlockSpec((1, tb, c), lambda b, m: (b, m, col))
    if mask_first:
        past = lambda col: pl.BlockSpec((1, BAND_PAST, c), lambda b, m: (b, jnp.maximum(m - 1, 0), col))
        past_specs = [past(3), past(4)]
    else:
        cache = pl.BlockSpec((None, 1, BAND_PAST, c), lambda b, m: (l, b, 0, 0))
        past_specs = [cache, cache]
    return pl.pallas_call(
        functools.partial(_attn_kernel, tq=tq, mask_first=mask_first),
        grid=(bsz, t // tb),
        in_specs=[zcol(2), past_specs[0], zcol(3), past_specs[1], zcol(4), _slab(evec, l)],
        out_specs=pl.BlockSpec((1, tb, c), lambda b, m: (b, m, 0)),
        out_shape=jax.ShapeDtypeStruct((bsz, t, c), BF16),
        scratch_shapes=[pltpu.VMEM((N_HEADS // 2, 2 * tq, win), F32),
                        pltpu.VMEM((2, nq, 2 * tq, win), F32),
                        pltpu.VMEM((2, nq, 2 * tq, win), BF16)],
        compiler_params=_params(2),
        name="band_attention",
    )(z, k_past, z, v_past, z, evec)


def _pool_kernel(p_ref, hist_ref, w_ref, scale_ref, y_ref, ext_ref, *, pos0):
    t = pl.program_id(1)
    tm = p_ref.shape[1]
    c = p_ref.shape[2]
    pad = POOL_HIST_PAD - POOL_HIST

    @pl.when(t == 0)
    def _():
        ext_ref[0:pad, :] = jnp.zeros((pad, c), F32)
        ext_ref[pad:POOL_HIST_PAD, :] = hist_ref[0]

    @pl.when(t > 0)
    def _():
        ext_ref[0:POOL_HIST_PAD, :] = ext_ref[tm:tm + POOL_HIST_PAD, :]

    ext_ref[POOL_HIST_PAD:POOL_HIST_PAD + tm, :] = p_ref[0]
    pos = pos0 + t * tm + lax.broadcasted_iota(jnp.int32, (tm, 1), 0)
    gw = c // len(POOL_WINDOWS)
    for gi, w in enumerate(POOL_WINDOWS):
        cs = slice(gi * gw, (gi + 1) * gw)
        run = ext_ref[:, cs]
        shift = 1
        while shift < w:
            run = run + pltpu.roll(run, shift, 0)
            shift *= 2
        cur = ext_ref[POOL_HIST_PAD:POOL_HIST_PAD + tm, cs]
        win = run[POOL_HIST_PAD:POOL_HIST_PAD + tm]
        cnt = jnp.minimum(w, pos + 1).astype(F32)
        mgrp = (win / cnt - cur).astype(BF16)
        y = jnp.dot(mgrp, w_ref[gi], preferred_element_type=F32) * scale_ref[:, cs]
        y_ref[0, :, cs] = y.astype(y_ref.dtype)


def _sg_kernel(u_ref, v_ref, g_ref, b_ref, w_ref, bias_ref, y_ref, *vn_out):
    tm = u_ref.shape[1]
    c = u_ref.shape[2]
    span = bias_ref.shape[0]
    gw = c // SG_GROUPS
    v = v_ref[0]
    mu = jnp.mean(v, axis=-1, keepdims=True)
    dlt = v - mu
    var = jnp.mean(dlt * dlt, axis=-1, keepdims=True)
    vn = dlt * lax.rsqrt(var + EPS) * g_ref[...] + b_ref[...]
    if vn_out:
        vn_out[0][0] = vn
    vb = vn.astype(BF16)
    row = lax.broadcasted_iota(jnp.int32, (span, span), 0)
    colm = lax.broadcasted_iota(jnp.int32, (span, span), 1)
    for g in range(SG_GROUPS):
        cs = slice(g * gw, (g + 1) * gw)
        wg = jnp.where(colm <= row, w_ref[g, :span, :span], 0.0).astype(BF16)
        for ci in range(tm // span):
            rs = slice(ci * span, (ci + 1) * span)
            s = jnp.dot(wg, vb[rs, cs], preferred_element_type=F32) + bias_ref[:, cs]
            y_ref[0, rs, cs] = (u_ref[0, rs, cs] * s).astype(y_ref.dtype)


def _row_mixers_kernel(za_ref, zg_ref, zp_ref, zu_ref, zs_ref, chist_ref, phist_ref,
                       cw_ref, cb_ref, gg_ref, gb_ref, avg_ref, pw_ref, poolw_ref, pscale_ref,
                       lng_ref, lnb_ref, sgw_ref, sgbias_ref,
                       ya_ref, cstate_ref, yc_ref, yd_ref, *rest, pos0, emit_vn):
    vn_out = rest[:1] if emit_vn else ()
    cext_ref, cacc_ref, pext_ref = rest[len(vn_out):]
    _conv_kernel(za_ref, zg_ref, chist_ref, cw_ref, cb_ref, gg_ref, gb_ref, avg_ref, pw_ref,
                 ya_ref, cstate_ref, cext_ref, cacc_ref)
    _pool_kernel(zp_ref, phist_ref, poolw_ref, pscale_ref, yc_ref, pext_ref, pos0=pos0)
    _sg_kernel(zu_ref, zs_ref, lng_ref, lnb_ref, sgw_ref, sgbias_ref, yd_ref, *vn_out)


def _row_mixers(z, conv_hist, pool_hist, hist_layer, stacked, avg, sg_bias, l, tm, pos0, emit_vn):
    bsz, t, _ = z.shape
    c = avg.shape[0]
    conv_w, conv_b, gn_g, gn_b, conv_pw, pool_w, pool_scale, ln_g, ln_b, sg_w = stacked
    zcol = lambda col: pl.BlockSpec((1, tm, c), lambda b, i: (b, i, col))
    hist = lambda rows: pl.BlockSpec((None, 1, rows, c), lambda b, i: (hist_layer, b, 0, 0))
    tile = pl.BlockSpec((1, tm, c), lambda b, i: (b, i, 0))
    span = min(tm, SG_CHUNK)
    params = [(conv_w, _slab(conv_w, l)), (conv_b, _slab(conv_b, l)), (gn_g, _slab(gn_g, l)),
              (gn_b, _slab(gn_b, l)), (avg, pl.BlockSpec(avg.shape, lambda b, i: (0, 0))),
              (conv_pw, _slab(conv_pw, l)), (pool_w, _slab(pool_w, l)), (pool_scale, _slab(pool_scale, l)),
              (ln_g, _slab(ln_g, l)), (ln_b, _slab(ln_b, l)), (sg_w, _slab(sg_w, l)),
              (sg_bias, pl.BlockSpec((None, span, c), lambda b, i: (l, 0, 0)))]
    out_specs = [tile, pl.BlockSpec((1, CONV_W - 1, c), lambda b, i: (b, 0, 0)), tile, tile]
    out_shape = [jax.ShapeDtypeStruct((bsz, t, c), BF16), jax.ShapeDtypeStruct((bsz, CONV_W - 1, c), F32),
                 jax.ShapeDtypeStruct((bsz, t, c), BF16), jax.ShapeDtypeStruct((bsz, t, c), BF16)]
    if emit_vn:
        out_specs.append(tile)
        out_shape.append(jax.ShapeDtypeStruct((bsz, t, c), F32))
    return pl.pallas_call(
        functools.partial(_row_mixers_kernel, pos0=pos0, emit_vn=emit_vn),
        grid=(bsz, t // tm),
        in_specs=[zcol(0), zcol(1), zcol(5), zcol(6), zcol(7), hist(CONV_W - 1), hist(POOL_HIST)]
                 + [spec for _, spec in params],
        out_specs=out_specs,
        out_shape=out_shape,
        scratch_shapes=[pltpu.VMEM((tm + CONV_HIST_PAD, c), F32), pltpu.VMEM((tm, c), F32),
                        pltpu.VMEM((tm + POOL_HIST_PAD, c), F32)],
        compiler_params=_params(2),
        name="row_mixers",
    )(z, z, z, z, z, conv_hist, pool_hist, *[arr for arr, _ in params])


def _outproj_kernel(x_ref, ya_ref, yb_ref, yc_ref, yd_ref, g1_ref, w_ref, n2_ref, sc2_ref, sh2_ref, o_ref, h_ref):
    tm = x_ref.shape[1]
    half = tm // 2 if tm % 16 == 0 else tm
    for r in range(tm // half):
        rs = slice(r * half, (r + 1) * half)
        mix = jnp.concatenate([ya_ref[0, rs], yb_ref[0, rs], yc_ref[0, rs], yd_ref[0, rs]], axis=-1)
        mod = lambda ref: ref[0] if ref.shape[1] == 1 else ref[0, rs]
        x1 = x_ref[0, rs] + mod(g1_ref) * jnp.dot(mix, w_ref[...], preferred_element_type=F32)
        o_ref[0, rs] = x1
        h_ref[0, rs] = (_rms(x1, n2_ref[...]) * (1.0 + mod(sc2_ref)) + mod(sh2_ref)).astype(BF16)


def _out_proj(x, ys, mods, w, norm_g, l, tm):
    bsz, t, d = x.shape
    c = ys[0].shape[2]
    mod_arr, mod_spec = mods
    yspec = pl.BlockSpec((1, tm, c), lambda b, i: (b, i, 0))
    xspec = pl.BlockSpec((1, tm, d), lambda b, i: (b, i, 0))
    return pl.pallas_call(
        _outproj_kernel,
        grid=(bsz, t // tm),
        in_specs=[xspec, yspec, yspec, yspec, yspec, mod_spec(G1, tm, _grid_bt),
                  _slab(w, l, single_buffer=True), _slab(norm_g, l),
                  mod_spec(SC2, tm, _grid_bt), mod_spec(SH2, tm, _grid_bt)],
        out_specs=[xspec, xspec],
        out_shape=[jax.ShapeDtypeStruct((bsz, t, d), F32), jax.ShapeDtypeStruct((bsz, t, d), BF16)],
        compiler_params=_params(2),
        name="out_proj",
    )(x, *ys, mod_arr, w, norm_g, mod_arr, mod_arr)


def _ffn_kernel(x_hbm, h_ref, g2_ref, wg_ref, wu_ref, wd_ref, fg_ref, o_ref, x_buf, x_sem, *, final_norm):
    b, i, f = pl.program_id(0), pl.program_id(1), pl.program_id(2)
    tm = o_ref.shape[1]
    x_copy = pltpu.make_async_copy(x_hbm.at[b, pl.ds(pl.multiple_of(i * tm, tm), tm), :], x_buf, x_sem)

    def step(first):
        h = h_ref[0]
        gate = jnp.dot(h, wg_ref[...], preferred_element_type=F32)
        up = jnp.dot(h, wu_ref[...], preferred_element_type=F32)
        act = (gate * _sigmoid(gate) * up).astype(BF16)
        part = jnp.dot(act, wd_ref[...], preferred_element_type=F32)
        if first:
            o_ref[0] = part
        else:
            o_ref[0] += part

    @pl.when(f == 0)
    def _():
        x_copy.start()
        step(True)

    pl.when(f > 0)(functools.partial(step, False))

    @pl.when(f == pl.num_programs(2) - 1)
    def _():
        x_copy.wait()
        y = x_buf[...] + g2_ref[0] * o_ref[0]
        if final_norm:
            y = _rms(y, fg_ref[...])
        o_ref[0] = y


def _ffn(x, h, mods, wg, wu, wd, fg, l, tm, final_norm):
    bsz, t, d = x.shape
    _, nf, _, tf = wg.shape
    mod_arr, mod_spec = mods
    tile = pl.BlockSpec((1, tm, d), lambda b, i, f: (b, i, 0))
    col_tile = pl.BlockSpec((None, None, d, tf), lambda b, i, f: (l, f, 0, 0))
    return pl.pallas_call(
        functools.partial(_ffn_kernel, final_norm=final_norm),
        grid=(bsz, t // tm, nf),
        in_specs=[pl.BlockSpec(memory_space=pl.ANY), tile, mod_spec(G2, tm, _grid_bt), col_tile, col_tile,
                  pl.BlockSpec((None, tf, d), lambda b, i, f: (l, f, 0)),
                  pl.BlockSpec((1, d), lambda b, i, f: (0, 0))],
        out_specs=tile,
        out_shape=jax.ShapeDtypeStruct((bsz, t, d), F32),
        scratch_shapes=[pltpu.VMEM((tm, d), F32), pltpu.SemaphoreType.DMA(())],
        compiler_params=_params(3),
        name="ffn",
    )(x, h, mod_arr, wg, wu, wd, fg)


def _col_tiles_kernel(a_ref, b_ref, oa_ref, ob_ref):
    oa_ref[...] = a_ref[...].astype(oa_ref.dtype)
    ob_ref[...] = b_ref[...].astype(ob_ref.dtype)


def _col_tiles(wa, wb, tf):
    depth, d, ff = wa.shape
    src = pl.BlockSpec((None, d, tf), lambda l, f: (l, 0, f))
    dst = pl.BlockSpec((None, None, d, tf), lambda l, f: (l, f, 0, 0))
    shape = jax.ShapeDtypeStruct((depth, ff // tf, d, tf), BF16)
    return pl.pallas_call(
        _col_tiles_kernel,
        grid=(depth, ff // tf),
        in_specs=[src, src],
        out_specs=[dst, dst],
        out_shape=[shape, shape],
        compiler_params=_params(2),
        name="ffn_weight_tiles",
    )(wa, wb)


def _largest_tile(t, cap):
    tile = min(t, cap)
    while t % tile:
        tile //= 2
    return tile


def _bias_vectors(rel_bias, tq, length=5 * LANES):
    off = tq - 1 + BAND_PAST
    assert length >= BAND_PAST + 2 * tq - 1
    head = min(length, off - REL_CLIP + 1)
    ramp_end = min(length, off + REL_CLIP + 1)
    lead = rel_bias.shape[:-1]
    parts = [jnp.broadcast_to(rel_bias[..., :1], lead + (head,)), rel_bias[..., 1:1 + ramp_end - head],
             jnp.broadcast_to(rel_bias[..., -1:], lead + (length - ramp_end,))]
    return jnp.concatenate(parts, axis=-1)


def kernel(x_prompt, x_sample, c_prompt, c_sample, cache_conv, cache_k, cache_v, cache_pool, ada_w, ada_b, norm1_g, norm2_g, w_in, conv_w, conv_b, conv_gn_g, conv_gn_b, conv_pw, rel_bias, pool_w, pool_scale, sg_ln_g, sg_ln_b, sg_w, sg_b, w_out, ffn_gate, ffn_up, ffn_down, final_g):
    depth = ada_w.shape[0]
    bp, tp, d = x_prompt.shape
    bs, ts, _ = x_sample.shape
    dc = conv_w.shape[2]
    ns = bs * ts
    assert tp % (CHUNK * 8) == 0 and ts <= CHUNK and ts <= SG_CHUNK and cache_k.shape[2] == BAND_PAST

    rows = bp + bs
    rows_pad = -(-rows // 8) * 8
    c_all = jnp.concatenate([c_prompt, c_sample, jnp.zeros((rows_pad - rows, d), F32)], axis=0)
    mod = _modulation(c_all, ada_w, ada_b)
    mod_s = mod[:, bp:rows].reshape(depth, bs, 6, d).transpose(0, 2, 1, 3)
    mod_s = jnp.repeat(mod_s, ts, axis=2).reshape(depth * 6, ns, d)

    w_in_b, w_out_b, wd_b = w_in.astype(BF16), w_out.astype(BF16), ffn_down.astype(BF16)
    wg_b, wu_b = _col_tiles(ffn_gate, ffn_up, FFN_TILE)
    vec = lambda a: a.reshape(depth, 1, -1)
    norm1, norm2 = vec(norm1_g), vec(norm2_g)
    mixer_w = (conv_w, vec(conv_b), vec(conv_gn_g), vec(conv_gn_b), conv_pw.astype(BF16), pool_w.astype(BF16),
               vec(pool_scale), vec(sg_ln_g), vec(sg_ln_b), sg_w)
    gsz = dc // CONV_GROUPS
    gid = jnp.arange(dc) // gsz
    avg = jnp.where(gid[:, None] == gid[None, :], 1.0 / gsz, 0.0).astype(BF16)
    sg_bias = jnp.repeat(sg_b.transpose(0, 2, 1), dc // SG_GROUPS, axis=2)
    evec_p, evec_s = _bias_vectors(rel_bias, CHUNK), _bias_vectors(rel_bias, ts)
    zero_conv = jnp.zeros((1, bp, CONV_W - 1, dc), F32)
    zero_pool = jnp.zeros((1, bp, POOL_HIST, dc), F32)
    cache_k2 = cache_k.reshape(depth, bs, BAND_PAST, dc)
    cache_v2 = cache_v.reshape(depth, bs, BAND_PAST, dc)
    fg = final_g.reshape(1, d)

    tm_p = _largest_tile(tp, 512)
    tm_f = _largest_tile(tp, 1024)
    xp = x_prompt
    xs = x_sample.reshape(1, ns, d)
    outs = {k: [] for k in ("conv_p", "conv_s", "k_p", "v_p", "k_s", "v_s", "pool_p", "pool_s", "sgv_s")}

    for l in range(depth):
        mods_p, mods_s = _seq_mods(mod, l), _row_mods(mod_s, l)
        final = l == depth - 1

        z = _in_proj(xp, norm1, mods_p, w_in_b, l, tm_p)
        ya, conv_state, yc, yd = _row_mixers(z, zero_conv, zero_pool, 0, mixer_w, avg, sg_bias, l, tm_p, 0, False)
        yb = _attention(z, z, z, evec_p, l, CHUNK, BAND_PAST // CHUNK, True)
        xp, hp = _out_proj(xp, (ya, yb, yc, yd), mods_p, w_out_b, norm2, l, tm_p)
        xp = _ffn(xp, hp, mods_p, wg_b, wu_b, wd_b, fg, l, tm_f, final)
        outs["conv_p"].append(conv_state)
        outs["k_p"].append(z[:, tp - BAND_PAST:, 3 * dc:4 * dc].reshape(bp, BAND_PAST, N_HEADS, -1))
        outs["v_p"].append(z[:, tp - BAND_PAST:, 4 * dc:5 * dc].reshape(bp, BAND_PAST, N_HEADS, -1))
        outs["pool_p"].append(z[:, tp - POOL_HIST:, 5 * dc:6 * dc])

        zs = _in_proj(xs, norm1, mods_s, w_in_b, l, ns).reshape(bs, ts, -1)
        ya, conv_state, yc, yd, sgv = _row_mixers(zs, cache_conv, cache_pool, l, mixer_w, avg, sg_bias, l, ts,
                                                  PAST_LEN, True)
        yb = _attention(zs, cache_k2, cache_v2, evec_s, l, ts, 1, False)
        ys = tuple(y.reshape(1, ns, dc) for y in (ya, yb, yc, yd))
        xs, hs = _out_proj(xs, ys, mods_s, w_out_b, norm2, l, ns)
        xs = _ffn(xs, hs, mods_s, wg_b, wu_b, wd_b, fg, l, ns, final)
        outs["conv_s"].append(conv_state)
        outs["k_s"].append(zs[:, :, 3 * dc:4 * dc].reshape(bs, ts, N_HEADS, -1))
        outs["v_s"].append(zs[:, :, 4 * dc:5 * dc].reshape(bs, ts, N_HEADS, -1))
        outs["pool_s"].append(jnp.concatenate([cache_pool[l], zs[:, :, 5 * dc:6 * dc]], axis=1)[:, -POOL_HIST:])
        outs["sgv_s"].append(sgv)

    st = {k: jnp.stack(v) for k, v in outs.items()}
    return (xp, xs.reshape(bs, ts, d), st["conv_p"], st["conv_s"], st["k_p"], st["v_p"], st["k_s"], st["v_s"],
            st["pool_p"], st["pool_s"], st["sgv_s"])
```

```python
import functools

import jax
import jax.numpy as jnp
from jax import lax
from jax.experimental import pallas as pl
from jax.experimental.pallas import tpu as pltpu

F32 = jnp.float32
BF16 = jnp.bfloat16

EPS = 1e-6
NEG_INF = -1e30
CHUNK = 64
BAND_PAST = 512
REL_CLIP = 128
N_HEADS = 8
CONV_W = 31
CONV_GROUPS = 8
POOL_WINDOWS = (2, 4, 8, 16)
POOL_HIST = 15
SG_CHUNK = 128
SG_GROUPS = 4
LANES = 128
SUBLANES = 8
CONV_HIST_PAD = 32
POOL_HIST_PAD = 16
CONV_ROWS = 256
FFN_TILE = 512
VMEM_LIMIT = 56 * 1024 * 1024
PAST_LEN = 2048


def _params(n_axes, vmem=VMEM_LIMIT):
    return pltpu.CompilerParams(dimension_semantics=("arbitrary",) * n_axes, vmem_limit_bytes=vmem)


def _sigmoid(x):
    return 1.0 / (1.0 + jnp.exp(-x))


def _rms(x, g):
    return x * lax.rsqrt(jnp.mean(x * x, axis=-1, keepdims=True) + EPS) * g


def _mod_kernel(c_ref, w_ref, b_ref, o_ref):
    c = c_ref[...]
    a = (c * _sigmoid(c)).astype(BF16)
    o_ref[0] = jnp.dot(a, w_ref[0].astype(BF16), preferred_element_type=F32) + b_ref[0]


def _modulation(c_all, ada_w, ada_b, tn=1536):
    depth, d, n = ada_w.shape
    rows = c_all.shape[0]
    return pl.pallas_call(
        _mod_kernel,
        grid=(depth, n // tn),
        in_specs=[pl.BlockSpec((rows, d), lambda l, j: (0, 0)),
                  pl.BlockSpec((1, d, tn), lambda l, j: (l, 0, j)),
                  pl.BlockSpec((1, 1, tn), lambda l, j: (l, 0, j))],
        out_specs=pl.BlockSpec((1, rows, tn), lambda l, j: (l, 0, j)),
        out_shape=jax.ShapeDtypeStruct((depth, rows, n), F32),
        compiler_params=_params(2),
        name="modulation",
    )(c_all, ada_w, ada_b.reshape(depth, 1, n))


def _inproj_kernel(x_ref, g_ref, sc_ref, sh_ref, w_ref, z_ref):
    tm = x_ref.shape[0]
    half = tm // 2 if tm % (2 * SUBLANES) == 0 else tm
    for r in range(tm // half):
        rs = slice(r * half, (r + 1) * half)
        mod = lambda ref: ref[0] if ref.shape[1] == 1 else ref[0, rs]
        h = _rms(x_ref[rs], g_ref[...]) * (1.0 + mod(sc_ref)) + mod(sh_ref)
        z_ref[rs] = jnp.dot(h.astype(BF16), w_ref[...], preferred_element_type=F32)


def _slab(arr, l, single_buffer=False):
    nd = arr.ndim - 1
    mode = dict(pipeline_mode=pl.Buffered(1)) if single_buffer else {}
    return pl.BlockSpec((None,) + arr.shape[1:], lambda *_: (l,) + (0,) * nd, **mode)


def _seq_mods(mod, l):
    depth, seqs, d6 = mod.shape
    d = d6 // 6
    flat = mod.reshape(depth * seqs * 6, 1, d)

    def spec(j, tm, locate):
        return pl.BlockSpec((1, 1, d), lambda *g: ((l * seqs + locate(*g)[0]) * 6 + j, 0, 0))
    return flat, spec


def _row_mods(mod_rows, l):
    d = mod_rows.shape[2]

    def spec(j, tm, locate):
        return pl.BlockSpec((1, tm, d), lambda *g: (l * 6 + j, locate(*g)[1], 0))
    return mod_rows, spec


def _grid_bt(b, t, *_):
    return b, t


SH1, SC1, G1, SH2, SC2, G2 = range(6)


def _in_proj(x, norm_g, mods, w, l, tm):
    bsz, t, d = x.shape
    n = w.shape[2]
    n_tiles = bsz * t // tm
    tiles_per_seq = t // tm
    mod_arr, mod_spec = mods
    locate = lambda i: (i // tiles_per_seq, i % tiles_per_seq)
    z = pl.pallas_call(
        _inproj_kernel,
        grid=(n_tiles,),
        in_specs=[pl.BlockSpec((tm, d), lambda i: (i, 0)),
                  _slab(norm_g, l), mod_spec(SC1, tm, locate), mod_spec(SH1, tm, locate),
                  _slab(w, l, single_buffer=True)],
        out_specs=pl.BlockSpec((tm, n), lambda i: (i, 0)),
        out_shape=jax.ShapeDtypeStruct((bsz * t, n), F32),
        compiler_params=_params(1),
        name="in_proj",
    )(x.reshape(bsz * t, d), norm_g, mod_arr, mod_arr, w)
    return z.reshape(bsz, t, n)


def _group_mean(v, avg_ref):
    hi = v.astype(BF16)
    lo = (v - hi.astype(F32)).astype(BF16)
    a = avg_ref[...]
    return jnp.dot(hi, a, preferred_element_type=F32) + jnp.dot(lo, a, preferred_element_type=F32)


def _conv_kernel(za_ref, zg_ref, hist_ref, w_ref, b_ref, gg_ref, gb_ref, avg_ref, pw_ref,
                 y_ref, state_ref, ext_ref, acc_ref):
    t = pl.program_id(1)
    tm = za_ref.shape[1]
    c = za_ref.shape[2]
    hist = CONV_W - 1
    pad = CONV_HIST_PAD - hist

    @pl.when(t == 0)
    def _():
        ext_ref[0:pad, :] = jnp.zeros((pad, c), F32)
        ext_ref[pad:CONV_HIST_PAD, :] = hist_ref[0]

    @pl.when(t > 0)
    def _():
        ext_ref[0:CONV_HIST_PAD, :] = ext_ref[tm:tm + CONV_HIST_PAD, :]

    ext_ref[CONV_HIST_PAD:CONV_HIST_PAD + tm, :] = za_ref[0] * _sigmoid(zg_ref[0])

    rows = min(CONV_ROWS, tm)
    for r in range(tm // rows):
        base = r * rows
        acc = jnp.broadcast_to(b_ref[...], (rows, c))
        for phase in range(SUBLANES):
            offs = [o for o in range(pad, pad + CONV_W) if o % SUBLANES == phase]
            span = rows + (SUBLANES if phase else 0)
            grp = None
            for o in offs:
                lo = base + o - phase
                term = ext_ref[lo:lo + span, :] * w_ref[o - pad:o - pad + 1, :]
                grp = term if grp is None else grp + term
            acc = acc + grp[phase:phase + rows]
        acc_ref[base:base + rows, :] = acc

    y = acc_ref[...]
    mu = _group_mean(y, avg_ref)
    dlt = y - mu
    var = _group_mean(dlt * dlt, avg_ref)
    yn = dlt * lax.rsqrt(var + EPS) * gg_ref[...] + gb_ref[...]
    s = yn * _sigmoid(yn)
    y_ref[0] = jnp.dot(s.astype(BF16), pw_ref[...], preferred_element_type=F32).astype(y_ref.dtype)

    @pl.when(t == pl.num_programs(1) - 1)
    def _():
        state_ref[0] = ext_ref[tm + pad:tm + CONV_HIST_PAD, :]


def _attn_kernel(q_ref, kp_ref, kc_ref, vp_ref, vc_ref, e_ref, o_ref, bias_ref, s_ref, p_ref, *, tq, mask_first):
    first = (pl.program_id(0) == 0) & (pl.program_id(1) == 0)
    m = pl.program_id(1)
    nq = q_ref.shape[1] // tq
    past = kp_ref.shape[1]
    win = past + tq
    hd = LANES // 2
    elen = e_ref.shape[1]

    @pl.when(first)
    def _():
        for h in range(N_HEADS):
            rows = jnp.broadcast_to(e_ref[h:h + 1, :], (tq, elen))
            tz = pltpu.roll(rows, elen - (tq - 1), 1, stride=1, stride_axis=0)
            bias_ref[h // 2, (h % 2) * tq:(h % 2 + 1) * tq, :] = tz[:, :win]

    low = lax.broadcasted_iota(jnp.int32, (1, LANES), 1) < hd
    col = lax.broadcasted_iota(jnp.int32, (1, win), 1)

    def body(masked):
        for hp in range(q_ref.shape[2] // LANES):
            cs = slice(hp * LANES, (hp + 1) * LANES)
            buf = hp % 2
            q = (q_ref[0, :, cs] * (hd ** -0.5)).astype(BF16)
            qa = jnp.where(low, q, jnp.zeros_like(q))
            qb = jnp.where(low, jnp.zeros_like(q), q)
            kk = jnp.concatenate([kp_ref[0, :, cs], kc_ref[0, :, cs]], axis=0).astype(BF16)
            vv = jnp.concatenate([vp_ref[0, :, cs], vc_ref[0, :, cs]], axis=0).astype(BF16)
            for c in range(nq):
                q2 = jnp.concatenate([qa[c * tq:(c + 1) * tq], qb[c * tq:(c + 1) * tq]], axis=0)
                s = lax.dot_general(q2, kk[c * tq:c * tq + win], (((1,), (1,)), ((), ())),
                                    preferred_element_type=F32) + bias_ref[hp]
                if masked:
                    s = jnp.where(col >= past - c * tq, s, NEG_INF)
                s_ref[buf, c] = s
            inv = []
            for c in range(nq):
                s = s_ref[buf, c]
                p = jnp.exp(s - jnp.max(s, axis=-1, keepdims=True))
                inv.append(1.0 / jnp.sum(p, axis=-1, keepdims=True))
                p_ref[buf, c] = p.astype(BF16)
            for c in range(nq):
                o2 = jnp.dot(p_ref[buf, c], vv[c * tq:c * tq + win], preferred_element_type=F32) * inv[c]
                o = jnp.where(low, o2[:tq], o2[tq:])
                o_ref[0, c * tq:(c + 1) * tq, cs] = o.astype(o_ref.dtype)

    if mask_first:
        pl.when(m == 0)(functools.partial(body, True))
        pl.when(m > 0)(functools.partial(body, False))
    else:
        body(False)


def _attention(z, k_past, v_past, evec, l, tq, nq, mask_first):
    bsz, t, _ = z.shape
    c = N_HEADS * (LANES // 2)
    tb = tq * nq
    win = BAND_PAST + tq
    zcol = lambda col: pl.BlockSpec((1, tb, c), lambda b, m: (b, m, col))
    if mask_first:
        past = lambda col: pl.BlockSpec((1, BAND_PAST, c), lambda b, m: (b, jnp.maximum(m - 1, 0), col))
        past_specs = [past(3), past(4)]
    else:
        cache = pl.BlockSpec((None, 1, BAND_PAST, c), lambda b, m: (l, b, 0, 0))
        past_specs = [cache, cache]
    return pl.pallas_call(
        functools.partial(_attn_kernel, tq=tq, mask_first=mask_first),
        grid=(bsz, t // tb),
        in_specs=[zcol(2), past_specs[0], zcol(3), past_specs[1], zcol(4), _slab(evec, l)],
        out_specs=pl.BlockSpec((1, tb, c), lambda b, m: (b, m, 0)),
        out_shape=jax.ShapeDtypeStruct((bsz, t, c), BF16),
        scratch_shapes=[pltpu.VMEM((N_HEADS // 2, 2 * tq, win), F32),
                        pltpu.VMEM((2, nq, 2 * tq, win), F32),
                        pltpu.VMEM((2, nq, 2 * tq, win), BF16)],
        compiler_params=_params(2),
        name="band_attention",
    )(z, k_past, z, v_past, z, evec)


def _pool_kernel(p_ref, hist_ref, w_ref, scale_ref, y_ref, ext_ref, *, pos0):
    t = pl.program_id(1)
    tm = p_ref.shape[1]
    c = p_ref.shape[2]
    pad = POOL_HIST_PAD - POOL_HIST

    @pl.when(t == 0)
    def _():
        ext_ref[0:pad, :] = jnp.zeros((pad, c), F32)
        ext_ref[pad:POOL_HIST_PAD, :] = hist_ref[0]

    @pl.when(t > 0)
    def _():
        ext_ref[0:POOL_HIST_PAD, :] = ext_ref[tm:tm + POOL_HIST_PAD, :]

    ext_ref[POOL_HIST_PAD:POOL_HIST_PAD + tm, :] = p_ref[0]
    pos = pos0 + t * tm + lax.broadcasted_iota(jnp.int32, (tm, 1), 0)
    gw = c // len(POOL_WINDOWS)
    for gi, w in enumerate(POOL_WINDOWS):
        cs = slice(gi * gw, (gi + 1) * gw)
        run = ext_ref[:, cs]
        shift = 1
        while shift < w:
            run = run + pltpu.roll(run, shift, 0)
            shift *= 2
        cur = ext_ref[POOL_HIST_PAD:POOL_HIST_PAD + tm, cs]
        win = run[POOL_HIST_PAD:POOL_HIST_PAD + tm]
        cnt = jnp.minimum(w, pos + 1).astype(F32)
        mgrp = (win / cnt - cur).astype(BF16)
        y = jnp.dot(mgrp, w_ref[gi], preferred_element_type=F32) * scale_ref[:, cs]
        y_ref[0, :, cs] = y.astype(y_ref.dtype)


def _sg_kernel(u_ref, v_ref, g_ref, b_ref, w_ref, bias_ref, y_ref, *vn_out):
    tm = u_ref.shape[1]
    c = u_ref.shape[2]
    span = bias_ref.shape[0]
    gw = c // SG_GROUPS
    v = v_ref[0]
    mu = jnp.mean(v, axis=-1, keepdims=True)
    dlt = v - mu
    var = jnp.mean(dlt * dlt, axis=-1, keepdims=True)
    vn = dlt * lax.rsqrt(var + EPS) * g_ref[...] + b_ref[...]
    if vn_out:
        vn_out[0][0] = vn
    vb = vn.astype(BF16)
    row = lax.broadcasted_iota(jnp.int32, (span, span), 0)
    colm = lax.broadcasted_iota(jnp.int32, (span, span), 1)
    for g in range(SG_GROUPS):
        cs = slice(g * gw, (g + 1) * gw)
        wg = jnp.where(colm <= row, w_ref[g, :span, :span], 0.0).astype(BF16)
        for ci in range(tm // span):
            rs = slice(ci * span, (ci + 1) * span)
            s = jnp.dot(wg, vb[rs, cs], preferred_element_type=F32) + bias_ref[:, cs]
            y_ref[0, rs, cs] = (u_ref[0, rs, cs] * s).astype(y_ref.dtype)


def _row_mixers_kernel(za_ref, zg_ref, zp_ref, zu_ref, zs_ref, chist_ref, phist_ref,
                       cw_ref, cb_ref, gg_ref, gb_ref, avg_ref, pw_ref, poolw_ref, pscale_ref,
                       lng_ref, lnb_ref, sgw_ref, sgbias_ref,
                       ya_ref, cstate_ref, yc_ref, yd_ref, *rest, pos0, emit_vn):
    vn_out = rest[:1] if emit_vn else ()
    cext_ref, cacc_ref, pext_ref = rest[len(vn_out):]
    _conv_kernel(za_ref, zg_ref, chist_ref, cw_ref, cb_ref, gg_ref, gb_ref, avg_ref, pw_ref,
                 ya_ref, cstate_ref, cext_ref, cacc_ref)
    _pool_kernel(zp_ref, phist_ref, poolw_ref, pscale_ref, yc_ref, pext_ref, pos0=pos0)
    _sg_kernel(zu_ref, zs_ref, lng_ref, lnb_ref, sgw_ref, sgbias_ref, yd_ref, *vn_out)


def _row_mixers(z, conv_hist, pool_hist, hist_layer, stacked, avg, sg_bias, l, tm, pos0, emit_vn):
    bsz, t, _ = z.shape
    c = avg.shape[0]
    conv_w, conv_b, gn_g, gn_b, conv_pw, pool_w, pool_scale, ln_g, ln_b, sg_w = stacked
    zcol = lambda col: pl.BlockSpec((1, tm, c), lambda b, i: (b, i, col))
    hist = lambda rows: pl.BlockSpec((None, 1, rows, c), lambda b, i: (hist_layer, b, 0, 0))
    tile = pl.BlockSpec((1, tm, c), lambda b, i: (b, i, 0))
    span = min(tm, SG_CHUNK)
    params = [(conv_w, _slab(conv_w, l)), (conv_b, _slab(conv_b, l)), (gn_g, _slab(gn_g, l)),
              (gn_b, _slab(gn_b, l)), (avg, pl.BlockSpec(avg.shape, lambda b, i: (0, 0))),
              (conv_pw, _slab(conv_pw, l)), (pool_w, _slab(pool_w, l)), (pool_scale, _slab(pool_scale, l)),
              (ln_g, _slab(ln_g, l)), (ln_b, _slab(ln_b, l)), (sg_w, _slab(sg_w, l)),
              (sg_bias, pl.BlockSpec((None, span, c), lambda b, i: (l, 0, 0)))]
    out_specs = [tile, pl.BlockSpec((1, CONV_W - 1, c), lambda b, i: (b, 0, 0)), tile, tile]
    out_shape = [jax.ShapeDtypeStruct((bsz, t, c), BF16), jax.ShapeDtypeStruct((bsz, CONV_W - 1, c), F32),
                 jax.ShapeDtypeStruct((bsz, t, c), BF16), jax.ShapeDtypeStruct((bsz, t, c), BF16)]
    if emit_vn:
        out_specs.append(tile)
        out_shape.append(jax.ShapeDtypeStruct((bsz, t, c), F32))
    return pl.pallas_call(
        functools.partial(_row_mixers_kernel, pos0=pos0, emit_vn=emit_vn),
        grid=(bsz, t // tm),
        in_specs=[zcol(0), zcol(1), zcol(5), zcol(6), zcol(7), hist(CONV_W - 1), hist(POOL_HIST)]
                 + [spec for _, spec in params],
        out_specs=out_specs,
        out_shape=out_shape,
        scratch_shapes=[pltpu.VMEM((tm + CONV_HIST_PAD, c), F32), pltpu.VMEM((tm, c), F32),
                        pltpu.VMEM((tm + POOL_HIST_PAD, c), F32)],
        compiler_params=_params(2),
        name="row_mixers",
    )(z, z, z, z, z, conv_hist, pool_hist, *[arr for arr, _ in params])


def _outproj_kernel(x_ref, ya_ref, yb_ref, yc_ref, yd_ref, g1_ref, w_ref, n2_ref, sc2_ref, sh2_ref, o_ref, h_ref):
    tm = x_ref.shape[1]
    half = tm // 2 if tm % 16 == 0 else tm
    for r in range(tm // half):
        rs = slice(r * half, (r + 1) * half)
        mix = jnp.concatenate([ya_ref[0, rs], yb_ref[0, rs], yc_ref[0, rs], yd_ref[0, rs]], axis=-1)
        mod = lambda ref: ref[0] if ref.shape[1] == 1 else ref[0, rs]
        x1 = x_ref[0, rs] + mod(g1_ref) * jnp.dot(mix, w_ref[...], preferred_element_type=F32)
        o_ref[0, rs] = x1
        h_ref[0, rs] = (_rms(x1, n2_ref[...]) * (1.0 + mod(sc2_ref)) + mod(sh2_ref)).astype(BF16)


def _out_proj(x, ys, mods, w, norm_g, l, tm):
    bsz, t, d = x.shape
    c = ys[0].shape[2]
    mod_arr, mod_spec = mods
    yspec = pl.BlockSpec((1, tm, c), lambda b, i: (b, i, 0))
    xspec = pl.BlockSpec((1, tm, d), lambda b, i: (b, i, 0))
    return pl.pallas_call(
        _outproj_kernel,
        grid=(bsz, t // tm),
        in_specs=[xspec, yspec, yspec, yspec, yspec, mod_spec(G1, tm, _grid_bt),
                  _slab(w, l, single_buffer=True), _slab(norm_g, l),
                  mod_spec(SC2, tm, _grid_bt), mod_spec(SH2, tm, _grid_bt)],
        out_specs=[xspec, xspec],
        out_shape=[jax.ShapeDtypeStruct((bsz, t, d), F32), jax.ShapeDtypeStruct((bsz, t, d), BF16)],
        compiler_params=_params(2),
        name="out_proj",
    )(x, *ys, mod_arr, w, norm_g, mod_arr, mod_arr)


def _ffn_kernel(x_hbm, h_ref, g2_ref, wgu_ref, wd_ref, fg_ref, o_ref, x_buf, x_sem, *, final_norm):
    b, i, f = pl.program_id(0), pl.program_id(1), pl.program_id(2)
    tm = o_ref.shape[1]
    x_copy = pltpu.make_async_copy(x_hbm.at[b, pl.ds(pl.multiple_of(i * tm, tm), tm), :], x_buf, x_sem)

    def step(first):
        tf = wd_ref.shape[0]
        gu = jnp.dot(h_ref[0], wgu_ref[...], preferred_element_type=F32)
        gate, up = gu[:, :tf], gu[:, tf:]
        act = (gate * _sigmoid(gate) * up).astype(BF16)
        part = jnp.dot(act, wd_ref[...], preferred_element_type=F32)
        if first:
            o_ref[0] = part
        else:
            o_ref[0] += part

    @pl.when(f == 0)
    def _():
        x_copy.start()
        step(True)

    pl.when(f > 0)(functools.partial(step, False))

    @pl.when(f == pl.num_programs(2) - 1)
    def _():
        x_copy.wait()
        y = x_buf[...] + g2_ref[0] * o_ref[0]
        if final_norm:
            y = _rms(y, fg_ref[...])
        o_ref[0] = y


def _ffn(x, h, mods, wgu, wd, fg, l, tm, final_norm):
    bsz, t, d = x.shape
    _, nf, _, tf2 = wgu.shape
    tf = tf2 // 2
    mod_arr, mod_spec = mods
    tile = pl.BlockSpec((1, tm, d), lambda b, i, f: (b, i, 0))
    col_tile = pl.BlockSpec((None, None, d, tf2), lambda b, i, f: (l, f, 0, 0))
    return pl.pallas_call(
        functools.partial(_ffn_kernel, final_norm=final_norm),
        grid=(bsz, t // tm, nf),
        in_specs=[pl.BlockSpec(memory_space=pl.ANY), tile, mod_spec(G2, tm, _grid_bt), col_tile,
                  pl.BlockSpec((None, tf, d), lambda b, i, f: (l, f, 0)),
                  pl.BlockSpec((1, d), lambda b, i, f: (0, 0))],
        out_specs=tile,
        out_shape=jax.ShapeDtypeStruct((bsz, t, d), F32),
        scratch_shapes=[pltpu.VMEM((tm, d), F32), pltpu.SemaphoreType.DMA(())],
        compiler_params=_params(3),
        name="ffn",
    )(x, h, mod_arr, wgu, wd, fg)


def _col_tiles_kernel(a_ref, b_ref, o_ref):
    tf = a_ref.shape[1]
    o_ref[:, :tf] = a_ref[...].astype(o_ref.dtype)
    o_ref[:, tf:] = b_ref[...].astype(o_ref.dtype)


def _col_tiles(wa, wb, tf):
    depth, d, ff = wa.shape
    src = pl.BlockSpec((None, d, tf), lambda l, f: (l, 0, f))
    return pl.pallas_call(
        _col_tiles_kernel,
        grid=(depth, ff // tf),
        in_specs=[src, src],
        out_specs=pl.BlockSpec((None, None, d, 2 * tf), lambda l, f: (l, f, 0, 0)),
        out_shape=jax.ShapeDtypeStruct((depth, ff // tf, d, 2 * tf), BF16),
        compiler_params=_params(2),
        name="ffn_weight_tiles",
    )(wa, wb)


def _largest_tile(t, cap):
    tile = min(t, cap)
    while t % tile:
        tile //= 2
    return tile


def _bias_vectors(rel_bias, tq, length=5 * LANES):
    off = tq - 1 + BAND_PAST
    assert length >= BAND_PAST + 2 * tq - 1
    head = min(length, off - REL_CLIP + 1)
    ramp_end = min(length, off + REL_CLIP + 1)
    lead = rel_bias.shape[:-1]
    parts = [jnp.broadcast_to(rel_bias[..., :1], lead + (head,)), rel_bias[..., 1:1 + ramp_end - head],
             jnp.broadcast_to(rel_bias[..., -1:], lead + (length - ramp_end,))]
    return jnp.concatenate(parts, axis=-1)


def kernel(x_prompt, x_sample, c_prompt, c_sample, cache_conv, cache_k, cache_v, cache_pool, ada_w, ada_b, norm1_g, norm2_g, w_in, conv_w, conv_b, conv_gn_g, conv_gn_b, conv_pw, rel_bias, pool_w, pool_scale, sg_ln_g, sg_ln_b, sg_w, sg_b, w_out, ffn_gate, ffn_up, ffn_down, final_g):
    depth = ada_w.shape[0]
    bp, tp, d = x_prompt.shape
    bs, ts, _ = x_sample.shape
    dc = conv_w.shape[2]
    ns = bs * ts
    assert tp % (CHUNK * 8) == 0 and ts <= CHUNK and ts <= SG_CHUNK and cache_k.shape[2] == BAND_PAST

    rows = bp + bs
    rows_pad = -(-rows // 8) * 8
    c_all = jnp.concatenate([c_prompt, c_sample, jnp.zeros((rows_pad - rows, d), F32)], axis=0)
    mod = _modulation(c_all, ada_w, ada_b)
    mod_s = mod[:, bp:rows].reshape(depth, bs, 6, d).transpose(0, 2, 1, 3)
    mod_s = jnp.repeat(mod_s, ts, axis=2).reshape(depth * 6, ns, d)

    w_in_b, w_out_b, wd_b = w_in.astype(BF16), w_out.astype(BF16), ffn_down.astype(BF16)
    wgu_b = _col_tiles(ffn_gate, ffn_up, FFN_TILE)
    vec = lambda a: a.reshape(depth, 1, -1)
    norm1, norm2 = vec(norm1_g), vec(norm2_g)
    mixer_w = (conv_w, vec(conv_b), vec(conv_gn_g), vec(conv_gn_b), conv_pw.astype(BF16), pool_w.astype(BF16),
               vec(pool_scale), vec(sg_ln_g), vec(sg_ln_b), sg_w)
    gsz = dc // CONV_GROUPS
    gid = jnp.arange(dc) // gsz
    avg = jnp.where(gid[:, None] == gid[None, :], 1.0 / gsz, 0.0).astype(BF16)
    sg_bias = jnp.repeat(sg_b.transpose(0, 2, 1), dc // SG_GROUPS, axis=2)
    evec_p, evec_s = _bias_vectors(rel_bias, CHUNK), _bias_vectors(rel_bias, ts)
    zero_conv = jnp.zeros((1, bp, CONV_W - 1, dc), F32)
    zero_pool = jnp.zeros((1, bp, POOL_HIST, dc), F32)
    cache_k2 = cache_k.reshape(depth, bs, BAND_PAST, dc)
    cache_v2 = cache_v.reshape(depth, bs, BAND_PAST, dc)
    fg = final_g.reshape(1, d)

    tm_p = _largest_tile(tp, 512)
    tm_f = _largest_tile(tp, 1024)
    xp = x_prompt
    xs = x_sample.reshape(1, ns, d)
    outs = {k: [] for k in ("conv_p", "conv_s", "k_p", "v_p", "k_s", "v_s", "pool_p", "pool_s", "sgv_s")}

    for l in range(depth):
        mods_p, mods_s = _seq_mods(mod, l), _row_mods(mod_s, l)
        final = l == depth - 1

        z = _in_proj(xp, norm1, mods_p, w_in_b, l, tm_p)
        ya, conv_state, yc, yd = _row_mixers(z, zero_conv, zero_pool, 0, mixer_w, avg, sg_bias, l, tm_p, 0, False)
        yb = _attention(z, z, z, evec_p, l, CHUNK, BAND_PAST // CHUNK, True)
        xp, hp = _out_proj(xp, (ya, yb, yc, yd), mods_p, w_out_b, norm2, l, tm_p)
        xp = _ffn(xp, hp, mods_p, wgu_b, wd_b, fg, l, tm_f, final)
        outs["conv_p"].append(conv_state)
        outs["k_p"].append(z[:, tp - BAND_PAST:, 3 * dc:4 * dc].reshape(bp, BAND_PAST, N_HEADS, -1))
        outs["v_p"].append(z[:, tp - BAND_PAST:, 4 * dc:5 * dc].reshape(bp, BAND_PAST, N_HEADS, -1))
        outs["pool_p"].append(z[:, tp - POOL_HIST:, 5 * dc:6 * dc])

        zs = _in_proj(xs, norm1, mods_s, w_in_b, l, ns).reshape(bs, ts, -1)
        ya, conv_state, yc, yd, sgv = _row_mixers(zs, cache_conv, cache_pool, l, mixer_w, avg, sg_bias, l, ts,
                                                  PAST_LEN, True)
        yb = _attention(zs, cache_k2, cache_v2, evec_s, l, ts, 1, False)
        ys = tuple(y.reshape(1, ns, dc) for y in (ya, yb, yc, yd))
        xs, hs = _out_proj(xs, ys, mods_s, w_out_b, norm2, l, ns)
        xs = _ffn(xs, hs, mods_s, wgu_b, wd_b, fg, l, ns, final)
        outs["conv_s"].append(conv_state)
        outs["k_s"].append(zs[:, :, 3 * dc:4 * dc].reshape(bs, ts, N_HEADS, -1))
        outs["v_s"].append(zs[:, :, 4 * dc:5 * dc].reshape(bs, ts, N_HEADS, -1))
        outs["pool_s"].append(jnp.concatenate([cache_pool[l], zs[:, :, 5 * dc:6 * dc]], axis=1)[:, -POOL_HIST:])
        outs["sgv_s"].append(sgv)

    st = {k: jnp.stack(v) for k, v in outs.items()}
    return (xp, xs.reshape(bs, ts, d), st["conv_p"], st["conv_s"], st["k_p"], st["v_p"], st["k_s"], st["v_s"],
            st["pool_p"], st["pool_s"], st["sgv_s"])
```

```python
import functools

import jax
import jax.numpy as jnp
from jax import lax
from jax.experimental import pallas as pl
from jax.experimental.pallas import tpu as pltpu

F32 = jnp.float32
BF16 = jnp.bfloat16

EPS = 1e-6
NEG_INF = -1e30
CHUNK = 64
BAND_PAST = 512
REL_CLIP = 128
N_HEADS = 8
CONV_W = 31
CONV_GROUPS = 8
POOL_WINDOWS = (2, 4, 8, 16)
POOL_HIST = 15
SG_CHUNK = 128
SG_GROUPS = 4
LANES = 128
SUBLANES = 8
CONV_HIST_PAD = 32
POOL_HIST_PAD = 16
CONV_ROWS = 256
FFN_TILE = 512
VMEM_LIMIT = 56 * 1024 * 1024
PAST_LEN = 2048


def _params(n_axes, vmem=VMEM_LIMIT):
    return pltpu.CompilerParams(dimension_semantics=("arbitrary",) * n_axes, vmem_limit_bytes=vmem)


def _sigmoid(x):
    return 1.0 / (1.0 + jnp.exp(-x))


def _rms(x, g):
    return x * lax.rsqrt(jnp.mean(x * x, axis=-1, keepdims=True) + EPS) * g


def _mod_kernel(c_ref, w_ref, b_ref, o_ref):
    c = c_ref[...]
    a = (c * _sigmoid(c)).astype(BF16)
    o_ref[0] = jnp.dot(a, w_ref[0].astype(BF16), preferred_element_type=F32) + b_ref[0]


def _modulation(c_all, ada_w, ada_b, tn=1536):
    depth, d, n = ada_w.shape
    rows = c_all.shape[0]
    return pl.pallas_call(
        _mod_kernel,
        grid=(depth, n // tn),
        in_specs=[pl.BlockSpec((rows, d), lambda l, j: (0, 0)),
                  pl.BlockSpec((1, d, tn), lambda l, j: (l, 0, j)),
                  pl.BlockSpec((1, 1, tn), lambda l, j: (l, 0, j))],
        out_specs=pl.BlockSpec((1, rows, tn), lambda l, j: (l, 0, j)),
        out_shape=jax.ShapeDtypeStruct((depth, rows, n), F32),
        compiler_params=_params(2),
        name="modulation",
    )(c_all, ada_w, ada_b.reshape(depth, 1, n))


def _inproj_kernel(x_ref, g_ref, sc_ref, sh_ref, w_ref, z_ref):
    tm = x_ref.shape[0]
    half = tm // 2 if tm % (2 * SUBLANES) == 0 else tm
    for r in range(tm // half):
        rs = slice(r * half, (r + 1) * half)
        mod = lambda ref: ref[0] if ref.shape[1] == 1 else ref[0, rs]
        h = _rms(x_ref[rs], g_ref[...]) * (1.0 + mod(sc_ref)) + mod(sh_ref)
        z_ref[rs] = jnp.dot(h.astype(BF16), w_ref[...], preferred_element_type=F32)


def _slab(arr, l, single_buffer=False):
    nd = arr.ndim - 1
    mode = dict(pipeline_mode=pl.Buffered(1)) if single_buffer else {}
    return pl.BlockSpec((None,) + arr.shape[1:], lambda *_: (l,) + (0,) * nd, **mode)


def _seq_mods(mod, l):
    depth, seqs, d6 = mod.shape
    d = d6 // 6
    flat = mod.reshape(depth * seqs * 6, 1, d)

    def spec(j, tm, locate):
        return pl.BlockSpec((1, 1, d), lambda *g: ((l * seqs + locate(*g)[0]) * 6 + j, 0, 0))
    return flat, spec


def _row_mods(mod_rows, l):
    d = mod_rows.shape[2]

    def spec(j, tm, locate):
        return pl.BlockSpec((1, tm, d), lambda *g: (l * 6 + j, locate(*g)[1], 0))
    return mod_rows, spec


def _grid_bt(b, t, *_):
    return b, t


SH1, SC1, G1, SH2, SC2, G2 = range(6)


def _in_proj(x, norm_g, mods, w, l, tm):
    bsz, t, d = x.shape
    n = w.shape[2]
    n_tiles = bsz * t // tm
    tiles_per_seq = t // tm
    mod_arr, mod_spec = mods
    locate = lambda i: (i // tiles_per_seq, i % tiles_per_seq)
    z = pl.pallas_call(
        _inproj_kernel,
        grid=(n_tiles,),
        in_specs=[pl.BlockSpec((tm, d), lambda i: (i, 0)),
                  _slab(norm_g, l), mod_spec(SC1, tm, locate), mod_spec(SH1, tm, locate),
                  _slab(w, l, single_buffer=True)],
        out_specs=pl.BlockSpec((tm, n), lambda i: (i, 0)),
        out_shape=jax.ShapeDtypeStruct((bsz * t, n), F32),
        compiler_params=_params(1),
        name="in_proj",
    )(x.reshape(bsz * t, d), norm_g, mod_arr, mod_arr, w)
    return z.reshape(bsz, t, n)


def _group_mean(v, avg_ref):
    hi = v.astype(BF16)
    lo = (v - hi.astype(F32)).astype(BF16)
    a = avg_ref[...]
    return jnp.dot(hi, a, preferred_element_type=F32) + jnp.dot(lo, a, preferred_element_type=F32)


def _conv_kernel(za_ref, zg_ref, hist_ref, w_ref, b_ref, gg_ref, gb_ref, avg_ref, pw_ref,
                 y_ref, state_ref, ext_ref, acc_ref):
    t = pl.program_id(1)
    tm = za_ref.shape[1]
    c = za_ref.shape[2]
    hist = CONV_W - 1
    pad = CONV_HIST_PAD - hist

    @pl.when(t == 0)
    def _():
        ext_ref[0:pad, :] = jnp.zeros((pad, c), F32)
        ext_ref[pad:CONV_HIST_PAD, :] = hist_ref[0]

    @pl.when(t > 0)
    def _():
        ext_ref[0:CONV_HIST_PAD, :] = ext_ref[tm:tm + CONV_HIST_PAD, :]

    ext_ref[CONV_HIST_PAD:CONV_HIST_PAD + tm, :] = za_ref[0] * _sigmoid(zg_ref[0])

    rows = min(CONV_ROWS, tm)
    for r in range(tm // rows):
        base = r * rows
        acc = jnp.broadcast_to(b_ref[...], (rows, c))
        for phase in range(SUBLANES):
            offs = [o for o in range(pad, pad + CONV_W) if o % SUBLANES == phase]
            span = rows + (SUBLANES if phase else 0)
            grp = None
            for o in offs:
                lo = base + o - phase
                term = ext_ref[lo:lo + span, :] * w_ref[o - pad:o - pad + 1, :]
                grp = term if grp is None else grp + term
            acc = acc + grp[phase:phase + rows]
        acc_ref[base:base + rows, :] = acc

    y = acc_ref[...]
    mu = _group_mean(y, avg_ref)
    dlt = y - mu
    var = _group_mean(dlt * dlt, avg_ref)
    yn = dlt * lax.rsqrt(var + EPS) * gg_ref[...] + gb_ref[...]
    s = yn * _sigmoid(yn)
    y_ref[0] = jnp.dot(s.astype(BF16), pw_ref[...], preferred_element_type=F32).astype(y_ref.dtype)

    @pl.when(t == pl.num_programs(1) - 1)
    def _():
        state_ref[0] = ext_ref[tm + pad:tm + CONV_HIST_PAD, :]


def _attn_kernel(q_ref, kp_ref, kc_ref, vp_ref, vc_ref, e_ref, o_ref, bias_ref, s_ref, p_ref, *, tq, mask_first):
    first = (pl.program_id(0) == 0) & (pl.program_id(1) == 0)
    m = pl.program_id(1)
    nq = q_ref.shape[1] // tq
    past = kp_ref.shape[1]
    win = past + tq
    hd = LANES // 2
    elen = e_ref.shape[1]

    @pl.when(first)
    def _():
        for h in range(N_HEADS):
            rows = jnp.broadcast_to(e_ref[h:h + 1, :], (tq, elen))
            tz = pltpu.roll(rows, elen - (tq - 1), 1, stride=1, stride_axis=0)
            bias_ref[h // 2, (h % 2) * tq:(h % 2 + 1) * tq, :] = tz[:, :win]

    low = lax.broadcasted_iota(jnp.int32, (1, LANES), 1) < hd
    col = lax.broadcasted_iota(jnp.int32, (1, win), 1)

    def body(masked):
        for hp in range(q_ref.shape[2] // LANES):
            cs = slice(hp * LANES, (hp + 1) * LANES)
            buf = hp % 2
            q = (q_ref[0, :, cs] * (hd ** -0.5)).astype(BF16)
            qa = jnp.where(low, q, jnp.zeros_like(q))
            qb = jnp.where(low, jnp.zeros_like(q), q)
            kk = jnp.concatenate([kp_ref[0, :, cs], kc_ref[0, :, cs]], axis=0).astype(BF16)
            vv = jnp.concatenate([vp_ref[0, :, cs], vc_ref[0, :, cs]], axis=0).astype(BF16)
            for c in range(nq):
                q2 = jnp.concatenate([qa[c * tq:(c + 1) * tq], qb[c * tq:(c + 1) * tq]], axis=0)
                s = lax.dot_general(q2, kk[c * tq:c * tq + win], (((1,), (1,)), ((), ())),
                                    preferred_element_type=F32) + bias_ref[hp]
                if masked:
                    s = jnp.where(col >= past - c * tq, s, NEG_INF)
                s_ref[buf, c] = s
            inv = []
            for c in range(nq):
                s = s_ref[buf, c]
                p = jnp.exp(s - jnp.max(s, axis=-1, keepdims=True))
                inv.append(1.0 / jnp.sum(p, axis=-1, keepdims=True))
                p_ref[buf, c] = p.astype(BF16)
            for c in range(nq):
                o2 = jnp.dot(p_ref[buf, c], vv[c * tq:c * tq + win], preferred_element_type=F32) * inv[c]
                o = jnp.where(low, o2[:tq], o2[tq:])
                o_ref[0, c * tq:(c + 1) * tq, cs] = o.astype(o_ref.dtype)

    if mask_first:
        pl.when(m == 0)(functools.partial(body, True))
        pl.when(m > 0)(functools.partial(body, False))
    else:
        body(False)


def _attention(z, k_past, v_past, evec, l, tq, nq, mask_first):
    bsz, t, _ = z.shape
    c = N_HEADS * (LANES // 2)
    tb = tq * nq
    win = BAND_PAST + tq
    zcol = lambda col: pl.BlockSpec((1, tb, c), lambda b, m: (b, m, col))
    if mask_first:
        blocks = tb // BAND_PAST
        past = lambda col: pl.BlockSpec((1, BAND_PAST, c), lambda b, m: (b, jnp.maximum(m * blocks - 1, 0), col))
        past_specs = [past(3), past(4)]
    else:
        cache = pl.BlockSpec((None, 1, BAND_PAST, c), lambda b, m: (l, b, 0, 0))
        past_specs = [cache, cache]
    return pl.pallas_call(
        functools.partial(_attn_kernel, tq=tq, mask_first=mask_first),
        grid=(bsz, t // tb),
        in_specs=[zcol(2), past_specs[0], zcol(3), past_specs[1], zcol(4), _slab(evec, l)],
        out_specs=pl.BlockSpec((1, tb, c), lambda b, m: (b, m, 0)),
        out_shape=jax.ShapeDtypeStruct((bsz, t, c), BF16),
        scratch_shapes=[pltpu.VMEM((N_HEADS // 2, 2 * tq, win), F32),
                        pltpu.VMEM((2, nq, 2 * tq, win), F32),
                        pltpu.VMEM((2, nq, 2 * tq, win), BF16)],
        compiler_params=_params(2),
        name="band_attention",
    )(z, k_past, z, v_past, z, evec)


def _pool_kernel(p_ref, hist_ref, w_ref, scale_ref, y_ref, ext_ref, *, pos0):
    t = pl.program_id(1)
    tm = p_ref.shape[1]
    c = p_ref.shape[2]
    pad = POOL_HIST_PAD - POOL_HIST

    @pl.when(t == 0)
    def _():
        ext_ref[0:pad, :] = jnp.zeros((pad, c), F32)
        ext_ref[pad:POOL_HIST_PAD, :] = hist_ref[0]

    @pl.when(t > 0)
    def _():
        ext_ref[0:POOL_HIST_PAD, :] = ext_ref[tm:tm + POOL_HIST_PAD, :]

    ext_ref[POOL_HIST_PAD:POOL_HIST_PAD + tm, :] = p_ref[0]
    pos = pos0 + t * tm + lax.broadcasted_iota(jnp.int32, (tm, 1), 0)
    gw = c // len(POOL_WINDOWS)
    for gi, w in enumerate(POOL_WINDOWS):
        cs = slice(gi * gw, (gi + 1) * gw)
        run = ext_ref[:, cs]
        shift = 1
        while shift < w:
            run = run + pltpu.roll(run, shift, 0)
            shift *= 2
        cur = ext_ref[POOL_HIST_PAD:POOL_HIST_PAD + tm, cs]
        win = run[POOL_HIST_PAD:POOL_HIST_PAD + tm]
        cnt = jnp.minimum(w, pos + 1).astype(F32)
        mgrp = (win / cnt - cur).astype(BF16)
        y = jnp.dot(mgrp, w_ref[gi], preferred_element_type=F32) * scale_ref[:, cs]
        y_ref[0, :, cs] = y.astype(y_ref.dtype)


def _sg_kernel(u_ref, v_ref, g_ref, b_ref, w_ref, bias_ref, y_ref, *vn_out):
    tm = u_ref.shape[1]
    c = u_ref.shape[2]
    span = bias_ref.shape[0]
    gw = c // SG_GROUPS
    v = v_ref[0]
    mu = jnp.mean(v, axis=-1, keepdims=True)
    dlt = v - mu
    var = jnp.mean(dlt * dlt, axis=-1, keepdims=True)
    vn = dlt * lax.rsqrt(var + EPS) * g_ref[...] + b_ref[...]
    if vn_out:
        vn_out[0][0] = vn
    vb = vn.astype(BF16)
    row = lax.broadcasted_iota(jnp.int32, (span, span), 0)
    colm = lax.broadcasted_iota(jnp.int32, (span, span), 1)
    for g in range(SG_GROUPS):
        cs = slice(g * gw, (g + 1) * gw)
        wg = jnp.where(colm <= row, w_ref[g, :span, :span], 0.0).astype(BF16)
        for ci in range(tm // span):
            rs = slice(ci * span, (ci + 1) * span)
            s = jnp.dot(wg, vb[rs, cs], preferred_element_type=F32) + bias_ref[:, cs]
            y_ref[0, rs, cs] = (u_ref[0, rs, cs] * s).astype(y_ref.dtype)


def _row_mixers_kernel(za_ref, zg_ref, zp_ref, zu_ref, zs_ref, chist_ref, phist_ref,
                       cw_ref, cb_ref, gg_ref, gb_ref, avg_ref, pw_ref, poolw_ref, pscale_ref,
                       lng_ref, lnb_ref, sgw_ref, sgbias_ref,
                       ya_ref, cstate_ref, yc_ref, yd_ref, *rest, pos0, emit_vn):
    vn_out = rest[:1] if emit_vn else ()
    cext_ref, cacc_ref, pext_ref = rest[len(vn_out):]
    _conv_kernel(za_ref, zg_ref, chist_ref, cw_ref, cb_ref, gg_ref, gb_ref, avg_ref, pw_ref,
                 ya_ref, cstate_ref, cext_ref, cacc_ref)
    _pool_kernel(zp_ref, phist_ref, poolw_ref, pscale_ref, yc_ref, pext_ref, pos0=pos0)
    _sg_kernel(zu_ref, zs_ref, lng_ref, lnb_ref, sgw_ref, sgbias_ref, yd_ref, *vn_out)


def _row_mixers(z, conv_hist, pool_hist, hist_layer, stacked, avg, sg_bias, l, tm, pos0, emit_vn):
    bsz, t, _ = z.shape
    c = avg.shape[0]
    conv_w, conv_b, gn_g, gn_b, conv_pw, pool_w, pool_scale, ln_g, ln_b, sg_w = stacked
    zcol = lambda col: pl.BlockSpec((1, tm, c), lambda b, i: (b, i, col))
    hist = lambda rows: pl.BlockSpec((None, 1, rows, c), lambda b, i: (hist_layer, b, 0, 0))
    tile = pl.BlockSpec((1, tm, c), lambda b, i: (b, i, 0))
    span = min(tm, SG_CHUNK)
    params = [(conv_w, _slab(conv_w, l)), (conv_b, _slab(conv_b, l)), (gn_g, _slab(gn_g, l)),
              (gn_b, _slab(gn_b, l)), (avg, pl.BlockSpec(avg.shape, lambda b, i: (0, 0))),
              (conv_pw, _slab(conv_pw, l)), (pool_w, _slab(pool_w, l)), (pool_scale, _slab(pool_scale, l)),
              (ln_g, _slab(ln_g, l)), (ln_b, _slab(ln_b, l)), (sg_w, _slab(sg_w, l)),
              (sg_bias, pl.BlockSpec((None, span, c), lambda b, i: (l, 0, 0)))]
    out_specs = [tile, pl.BlockSpec((1, CONV_W - 1, c), lambda b, i: (b, 0, 0)), tile, tile]
    out_shape = [jax.ShapeDtypeStruct((bsz, t, c), BF16), jax.ShapeDtypeStruct((bsz, CONV_W - 1, c), F32),
                 jax.ShapeDtypeStruct((bsz, t, c), BF16), jax.ShapeDtypeStruct((bsz, t, c), BF16)]
    if emit_vn:
        out_specs.append(tile)
        out_shape.append(jax.ShapeDtypeStruct((bsz, t, c), F32))
    return pl.pallas_call(
        functools.partial(_row_mixers_kernel, pos0=pos0, emit_vn=emit_vn),
        grid=(bsz, t // tm),
        in_specs=[zcol(0), zcol(1), zcol(5), zcol(6), zcol(7), hist(CONV_W - 1), hist(POOL_HIST)]
                 + [spec for _, spec in params],
        out_specs=out_specs,
        out_shape=out_shape,
        scratch_shapes=[pltpu.VMEM((tm + CONV_HIST_PAD, c), F32), pltpu.VMEM((tm, c), F32),
                        pltpu.VMEM((tm + POOL_HIST_PAD, c), F32)],
        compiler_params=_params(2),
        name="row_mixers",
    )(z, z, z, z, z, conv_hist, pool_hist, *[arr for arr, _ in params])


def _outproj_kernel(x_ref, ya_ref, yb_ref, yc_ref, yd_ref, g1_ref, w_ref, n2_ref, sc2_ref, sh2_ref, o_ref, h_ref):
    tm = x_ref.shape[1]
    half = tm // 2 if tm % 16 == 0 else tm
    for r in range(tm // half):
        rs = slice(r * half, (r + 1) * half)
        mix = jnp.concatenate([ya_ref[0, rs], yb_ref[0, rs], yc_ref[0, rs], yd_ref[0, rs]], axis=-1)
        mod = lambda ref: ref[0] if ref.shape[1] == 1 else ref[0, rs]
        x1 = x_ref[0, rs] + mod(g1_ref) * jnp.dot(mix, w_ref[...], preferred_element_type=F32)
        o_ref[0, rs] = x1
        h_ref[0, rs] = (_rms(x1, n2_ref[...]) * (1.0 + mod(sc2_ref)) + mod(sh2_ref)).astype(BF16)


def _out_proj(x, ys, mods, w, norm_g, l, tm):
    bsz, t, d = x.shape
    c = ys[0].shape[2]
    mod_arr, mod_spec = mods
    yspec = pl.BlockSpec((1, tm, c), lambda b, i: (b, i, 0))
    xspec = pl.BlockSpec((1, tm, d), lambda b, i: (b, i, 0))
    return pl.pallas_call(
        _outproj_kernel,
        grid=(bsz, t // tm),
        in_specs=[xspec, yspec, yspec, yspec, yspec, mod_spec(G1, tm, _grid_bt),
                  _slab(w, l, single_buffer=True), _slab(norm_g, l),
                  mod_spec(SC2, tm, _grid_bt), mod_spec(SH2, tm, _grid_bt)],
        out_specs=[xspec, xspec],
        out_shape=[jax.ShapeDtypeStruct((bsz, t, d), F32), jax.ShapeDtypeStruct((bsz, t, d), BF16)],
        compiler_params=_params(2),
        name="out_proj",
    )(x, *ys, mod_arr, w, norm_g, mod_arr, mod_arr)


def _ffn_kernel(x_hbm, h_ref, g2_ref, wgu_ref, wd_ref, fg_ref, o_ref, x_buf, x_sem, *, final_norm):
    b, i, f = pl.program_id(0), pl.program_id(1), pl.program_id(2)
    tm = o_ref.shape[1]
    x_copy = pltpu.make_async_copy(x_hbm.at[b, pl.ds(pl.multiple_of(i * tm, tm), tm), :], x_buf, x_sem)

    def step(first):
        tf = wd_ref.shape[0]
        gu = jnp.dot(h_ref[0], wgu_ref[...], preferred_element_type=F32)
        gate, up = gu[:, :tf], gu[:, tf:]
        act = (gate * _sigmoid(gate) * up).astype(BF16)
        part = jnp.dot(act, wd_ref[...], preferred_element_type=F32)
        if first:
            o_ref[0] = part
        else:
            o_ref[0] += part

    @pl.when(f == 0)
    def _():
        x_copy.start()
        step(True)

    pl.when(f > 0)(functools.partial(step, False))

    @pl.when(f == pl.num_programs(2) - 1)
    def _():
        x_copy.wait()
        y = x_buf[...] + g2_ref[0] * o_ref[0]
        if final_norm:
            y = _rms(y, fg_ref[...])
        o_ref[0] = y


def _ffn(x, h, mods, wgu, wd, fg, l, tm, final_norm):
    bsz, t, d = x.shape
    _, nf, _, tf2 = wgu.shape
    tf = tf2 // 2
    mod_arr, mod_spec = mods
    tile = pl.BlockSpec((1, tm, d), lambda b, i, f: (b, i, 0))
    col_tile = pl.BlockSpec((None, None, d, tf2), lambda b, i, f: (l, f, 0, 0))
    return pl.pallas_call(
        functools.partial(_ffn_kernel, final_norm=final_norm),
        grid=(bsz, t // tm, nf),
        in_specs=[pl.BlockSpec(memory_space=pl.ANY), tile, mod_spec(G2, tm, _grid_bt), col_tile,
                  pl.BlockSpec((None, tf, d), lambda b, i, f: (l, f, 0)),
                  pl.BlockSpec((1, d), lambda b, i, f: (0, 0))],
        out_specs=tile,
        out_shape=jax.ShapeDtypeStruct((bsz, t, d), F32),
        scratch_shapes=[pltpu.VMEM((tm, d), F32), pltpu.SemaphoreType.DMA(())],
        compiler_params=_params(3),
        name="ffn",
    )(x, h, mod_arr, wgu, wd, fg)


def _col_tiles_kernel(a_ref, b_ref, o_ref):
    tf = a_ref.shape[1]
    o_ref[:, :tf] = a_ref[...].astype(o_ref.dtype)
    o_ref[:, tf:] = b_ref[...].astype(o_ref.dtype)


def _col_tiles(wa, wb, tf):
    depth, d, ff = wa.shape
    src = pl.BlockSpec((None, d, tf), lambda l, f: (l, 0, f))
    return pl.pallas_call(
        _col_tiles_kernel,
        grid=(depth, ff // tf),
        in_specs=[src, src],
        out_specs=pl.BlockSpec((None, None, d, 2 * tf), lambda l, f: (l, f, 0, 0)),
        out_shape=jax.ShapeDtypeStruct((depth, ff // tf, d, 2 * tf), BF16),
        compiler_params=_params(2),
        name="ffn_weight_tiles",
    )(wa, wb)


def _largest_tile(t, cap):
    tile = min(t, cap)
    while t % tile:
        tile //= 2
    return tile


def _bias_vectors(rel_bias, tq, length=5 * LANES):
    off = tq - 1 + BAND_PAST
    assert length >= BAND_PAST + 2 * tq - 1
    head = min(length, off - REL_CLIP + 1)
    ramp_end = min(length, off + REL_CLIP + 1)
    lead = rel_bias.shape[:-1]
    parts = [jnp.broadcast_to(rel_bias[..., :1], lead + (head,)), rel_bias[..., 1:1 + ramp_end - head],
             jnp.broadcast_to(rel_bias[..., -1:], lead + (length - ramp_end,))]
    return jnp.concatenate(parts, axis=-1)


def kernel(x_prompt, x_sample, c_prompt, c_sample, cache_conv, cache_k, cache_v, cache_pool, ada_w, ada_b, norm1_g, norm2_g, w_in, conv_w, conv_b, conv_gn_g, conv_gn_b, conv_pw, rel_bias, pool_w, pool_scale, sg_ln_g, sg_ln_b, sg_w, sg_b, w_out, ffn_gate, ffn_up, ffn_down, final_g):
    depth = ada_w.shape[0]
    bp, tp, d = x_prompt.shape
    bs, ts, _ = x_sample.shape
    dc = conv_w.shape[2]
    ns = bs * ts
    assert tp % (CHUNK * 8) == 0 and ts <= CHUNK and ts <= SG_CHUNK and cache_k.shape[2] == BAND_PAST

    rows = bp + bs
    rows_pad = -(-rows // 8) * 8
    c_all = jnp.concatenate([c_prompt, c_sample, jnp.zeros((rows_pad - rows, d), F32)], axis=0)
    mod = _modulation(c_all, ada_w, ada_b)
    mod_s = mod[:, bp:rows].reshape(depth, bs, 6, d).transpose(0, 2, 1, 3)
    mod_s = jnp.repeat(mod_s, ts, axis=2).reshape(depth * 6, ns, d)

    w_in_b, w_out_b, wd_b = w_in.astype(BF16), w_out.astype(BF16), ffn_down.astype(BF16)
    wgu_b = _col_tiles(ffn_gate, ffn_up, FFN_TILE)
    vec = lambda a: a.reshape(depth, 1, -1)
    norm1, norm2 = vec(norm1_g), vec(norm2_g)
    mixer_w = (conv_w, vec(conv_b), vec(conv_gn_g), vec(conv_gn_b), conv_pw.astype(BF16), pool_w.astype(BF16),
               vec(pool_scale), vec(sg_ln_g), vec(sg_ln_b), sg_w)
    gsz = dc // CONV_GROUPS
    gid = jnp.arange(dc) // gsz
    avg = jnp.where(gid[:, None] == gid[None, :], 1.0 / gsz, 0.0).astype(BF16)
    sg_bias = jnp.repeat(sg_b.transpose(0, 2, 1), dc // SG_GROUPS, axis=2)
    evec_p, evec_s = _bias_vectors(rel_bias, CHUNK), _bias_vectors(rel_bias, ts)
    zero_conv = jnp.zeros((1, bp, CONV_W - 1, dc), F32)
    zero_pool = jnp.zeros((1, bp, POOL_HIST, dc), F32)
    cache_k2 = cache_k.reshape(depth, bs, BAND_PAST, dc)
    cache_v2 = cache_v.reshape(depth, bs, BAND_PAST, dc)
    fg = final_g.reshape(1, d)

    tm_p = _largest_tile(tp, 512)
    tm_f = _largest_tile(tp, 1024)
    xp = x_prompt
    xs = x_sample.reshape(1, ns, d)
    outs = {k: [] for k in ("conv_p", "conv_s", "k_p", "v_p", "k_s", "v_s", "pool_p", "pool_s", "sgv_s")}

    for l in range(depth):
        mods_p, mods_s = _seq_mods(mod, l), _row_mods(mod_s, l)
        final = l == depth - 1

        z = _in_proj(xp, norm1, mods_p, w_in_b, l, tm_p)
        ya, conv_state, yc, yd = _row_mixers(z, zero_conv, zero_pool, 0, mixer_w, avg, sg_bias, l, tm_f, 0, False)
        yb = _attention(z, z, z, evec_p, l, CHUNK, tm_f // CHUNK, True)
        xp, hp = _out_proj(xp, (ya, yb, yc, yd), mods_p, w_out_b, norm2, l, tm_p)
        xp = _ffn(xp, hp, mods_p, wgu_b, wd_b, fg, l, tm_f, final)
        outs["conv_p"].append(conv_state)
        outs["k_p"].append(z[:, tp - BAND_PAST:, 3 * dc:4 * dc].reshape(bp, BAND_PAST, N_HEADS, -1))
        outs["v_p"].append(z[:, tp - BAND_PAST:, 4 * dc:5 * dc].reshape(bp, BAND_PAST, N_HEADS, -1))
        outs["pool_p"].append(z[:, tp - POOL_HIST:, 5 * dc:6 * dc])

        zs = _in_proj(xs, norm1, mods_s, w_in_b, l, ns).reshape(bs, ts, -1)
        ya, conv_state, yc, yd, sgv = _row_mixers(zs, cache_conv, cache_pool, l, mixer_w, avg, sg_bias, l, ts,
                                                  PAST_LEN, True)
        yb = _attention(zs, cache_k2, cache_v2, evec_s, l, ts, 1, False)
        ys = tuple(y.reshape(1, ns, dc) for y in (ya, yb, yc, yd))
        xs, hs = _out_proj(xs, ys, mods_s, w_out_b, norm2, l, ns)
        xs = _ffn(xs, hs, mods_s, wgu_b, wd_b, fg, l, ns, final)
        outs["conv_s"].append(conv_state)
        outs["k_s"].append(zs[:, :, 3 * dc:4 * dc].reshape(bs, ts, N_HEADS, -1))
        outs["v_s"].append(zs[:, :, 4 * dc:5 * dc].reshape(bs, ts, N_HEADS, -1))
        outs["pool_s"].append(jnp.concatenate([cache_pool[l], zs[:, :, 5 * dc:6 * dc]], axis=1)[:, -POOL_HIST:])
        outs["sgv_s"].append(sgv)

    st = {k: jnp.stack(v) for k, v in outs.items()}
    return (xp, xs.reshape(bs, ts, d), st["conv_p"], st["conv_s"], st["k_p"], st["v_p"], st["k_s"], st["v_s"],
            st["pool_p"], st["pool_s"], st["sgv_s"])
```
